```python
import math
import jax, jax.numpy as jnp
from jax import lax
import numpy as np

D_MODEL = 2048
BATCH = 4
SEQ = 4096
DEPTH = 1
DEC_BATCH = 128
DEC_SEQ = 8
PAST_LEN = 16384
PAGE_SIZE = 128

HEAD_DIM = 64
N_HEADS = 16
N_KV_HEADS = 4
GQA_GROUP = N_HEADS // N_KV_HEADS
ATTN_WIDTH = N_HEADS * HEAD_DIM
KV_WIDTH = N_KV_HEADS * HEAD_DIM
WINDOW = 128
NUM_BUCKETS = 32
MAX_DISTANCE = 128
GM_WIDTH = D_MODEL // 2
GM_GROUPS = 8
GM_GROUP_WIDTH = GM_WIDTH // GM_GROUPS
CHUNK = 128
N_EXPERTS = 32
TOP_K = 4
D_FF_EXPERT = D_MODEL
SWIGLU_ALPHA = 1.702
SWIGLU_LIMIT = 7.0
EXPERT_BLOCK = 256
NORM_EPS = 1e-5
NEG_INF = -1e30
IN_SPLITS = (ATTN_WIDTH, KV_WIDTH, KV_WIDTH, GM_WIDTH, GM_WIDTH, D_MODEL, D_MODEL)
IN_WIDTH = sum(IN_SPLITS)

kernel_name = "hybrid_swa_sink_chunk_gmlp_moe_step"


def rms_norm(x, g):
    xf = x.astype(jnp.float32)
    y = xf * lax.rsqrt(jnp.mean(xf * xf, axis=-1, keepdims=True) + NORM_EPS)
    return y.astype(x.dtype) * g


def layer_norm(x, g, b):
    xf = x.astype(jnp.float32)
    mu = jnp.mean(xf, axis=-1, keepdims=True)
    var = jnp.mean(jnp.square(xf - mu), axis=-1, keepdims=True)
    return ((xf - mu) * lax.rsqrt(var + NORM_EPS)).astype(x.dtype) * g + b


def rel_position_bucket(dist):
    n = jnp.maximum(dist, 0)
    max_exact = NUM_BUCKETS // 2
    nf = jnp.maximum(n, 1).astype(jnp.float32)
    large = max_exact + (jnp.log(nf / max_exact) / math.log(MAX_DISTANCE / max_exact)
                         * (NUM_BUCKETS - max_exact)).astype(jnp.int32)
    large = jnp.minimum(large, NUM_BUCKETS - 1)
    return jnp.where(n < max_exact, n, large)


def rel_bias_heads(dist, rel_bias):
    b = rel_bias[rel_position_bucket(dist)].astype(jnp.float32)
    b = jnp.moveaxis(b, -1, 0)
    return b.reshape((N_KV_HEADS, GQA_GROUP) + dist.shape)


def sink_softmax(scores, sink):
    sink = sink.astype(jnp.float32)
    m = jnp.maximum(jnp.max(scores, axis=-1, keepdims=True), sink)
    p = jnp.exp(scores - m)
    return p / (jnp.sum(p, axis=-1, keepdims=True) + jnp.exp(sink - m))


def swa_prompt(q, k, v, sinks, rel_bias):
    B, S = q.shape[0], q.shape[1]
    nb = S // WINDOW
    qb = q.reshape(B, nb, WINDOW, N_KV_HEADS, GQA_GROUP, HEAD_DIM)

    def with_prev(t):
        tb = t.reshape(B, nb, WINDOW, N_KV_HEADS, HEAD_DIM)
        prev = jnp.pad(tb, ((0, 0), (1, 0), (0, 0), (0, 0), (0, 0)))[:, :-1]
        return jnp.concatenate([prev, tb], axis=2)

    kb, vb = with_prev(k), with_prev(v)
    dist = WINDOW + jnp.arange(WINDOW)[:, None] - jnp.arange(2 * WINDOW)[None, :]
    key_pos = (jnp.arange(nb)[:, None] - 1) * WINDOW + jnp.arange(2 * WINDOW)[None, :]
    valid = (((dist >= 0) & (dist < WINDOW))[None, None, None]
             & (key_pos >= 0)[:, None, None, None, :])
    scale = HEAD_DIM ** -0.5
    scores = jnp.einsum('bnqkgd,bnskd->bnkgqs', qb, kb).astype(jnp.float32) * scale
    scores = jnp.where(valid, scores + rel_bias_heads(dist, rel_bias), NEG_INF)
    probs = sink_softmax(scores, sinks.reshape(N_KV_HEADS, GQA_GROUP, 1, 1))
    out = jnp.einsum('bnkgqs,bnskd->bnqkgd', probs.astype(vb.dtype), vb)
    return out.reshape(B, S, ATTN_WIDTH)


def swa_sample(q, k_new, v_new, k_buf, v_buf, sinks, rel_bias):
    Bd, Sd = q.shape[0], q.shape[1]
    L = k_buf.shape[1]
    kk = jnp.concatenate([k_buf, k_new], axis=1)
    vv = jnp.concatenate([v_buf, v_new], axis=1)
    qg = q.reshape(Bd, Sd, N_KV_HEADS, GQA_GROUP, HEAD_DIM)
    dist = L + jnp.arange(Sd)[:, None] - jnp.arange(L + Sd)[None, :]
    valid = (dist >= 0) & (dist < WINDOW)
    scale = HEAD_DIM ** -0.5
    scores = jnp.einsum('bqkgd,bskd->bkgqs', qg, kk).astype(jnp.float32) * scale
    scores = jnp.where(valid, scores + rel_bias_heads(dist, rel_bias), NEG_INF)
    probs = sink_softmax(scores, sinks.reshape(N_KV_HEADS, GQA_GROUP, 1, 1))
    out = jnp.einsum('bkgqs,bskd->bqkgd', probs.astype(vv.dtype), vv).reshape(Bd, Sd, ATTN_WIDTH)
    return out, kk[:, -L:], vv[:, -L:]


def chunk_spatial_gating(u, v, ln_g, ln_b, ws, bs):
    u = jax.nn.gelu(u)
    vn = layer_norm(jax.nn.gelu(v), ln_g, ln_b)
    B, S = u.shape[0], u.shape[1]
    pad = (-S) % CHUNK
    up = jnp.pad(u, ((0, 0), (0, pad), (0, 0)))
    vp = jnp.pad(vn, ((0, 0), (0, pad), (0, 0)))
    nc = (S + pad) // CHUNK
    vg = vp.reshape(B, nc, CHUNK, GM_GROUPS, GM_GROUP_WIDTH)
    ws_causal = jnp.where(jnp.tril(jnp.ones((CHUNK, CHUNK), dtype=bool)), ws, 0)
    mixed = jnp.einsum('gts,bcsgd->bctgd', ws_causal, vg) + bs.T[:, :, None]
    out = up * mixed.reshape(B, nc * CHUNK, GM_WIDTH)
    return out[:, :S], vn


def moe_ffn(h, router_w, router_b, w_gate_up, b_gate_up, w_down, b_down):
    T, D = h.shape
    A = T * TOP_K
    logits = (h @ router_w + router_b).astype(jnp.float32)
    top_logit, top_e = lax.top_k(logits, TOP_K)
    gate = jax.nn.softmax(top_logit, axis=-1).astype(h.dtype)
    flat_e = top_e.reshape(-1)
    order = jnp.argsort(flat_e)
    sorted_e = flat_e[order]
    sorted_tok = order // TOP_K
    sorted_gate = gate.reshape(-1)[order]
    counts = jnp.bincount(flat_e, length=N_EXPERTS)
    padded = (counts + EXPERT_BLOCK - 1) // EXPERT_BLOCK * EXPERT_BLOCK
    pad_end = jnp.cumsum(padded)
    pad_start = pad_end - padded
    start = jnp.cumsum(counts) - counts
    dest = pad_start[sorted_e] + (jnp.arange(A) - start[sorted_e])
    n_blocks = -(-A // EXPERT_BLOCK) + N_EXPERTS
    cap = n_blocks * EXPERT_BLOCK
    slot_tok = jnp.zeros((cap,), jnp.int32).at[dest].set(sorted_tok.astype(jnp.int32))
    slot_gate = jnp.zeros((cap,), h.dtype).at[dest].set(sorted_gate)
    block_e = jnp.minimum(jnp.searchsorted(pad_end, jnp.arange(n_blocks) * EXPERT_BLOCK, side='right'),
                          N_EXPERTS - 1)
    xs = h[slot_tok].reshape(n_blocks, EXPERT_BLOCK, D)

    def expert_block(args):
        xb, e = args
        gu = xb @ w_gate_up[e] + b_gate_up[e]
        glu, lin = jnp.split(gu, 2, axis=-1)
        glu = jnp.minimum(glu, SWIGLU_LIMIT)
        lin = jnp.clip(lin, -SWIGLU_LIMIT, SWIGLU_LIMIT)
        act = glu * jax.nn.sigmoid(SWIGLU_ALPHA * glu) * (lin + 1)
        return act @ w_down[e] + b_down[e]

    yb = lax.map(expert_block, (xs, block_e)).reshape(cap, D) * slot_gate[:, None]
    return jnp.zeros_like(h).at[slot_tok].add(yb)


def mixer_project(x, norm1_g, w_in):
    xn = rms_norm(x, norm1_g)
    z = xn @ w_in
    q, k, v, gu, gv, ga, gg = jnp.split(z, np.cumsum(IN_SPLITS)[:-1].tolist(), axis=-1)
    B, S = x.shape[0], x.shape[1]
    q = q.reshape(B, S, N_HEADS, HEAD_DIM)
    k = k.reshape(B, S, N_KV_HEADS, HEAD_DIM)
    v = v.reshape(B, S, N_KV_HEADS, HEAD_DIM)
    return q, k, v, gu, gv, ga, gg


def block_tail(x, attn_o, gm_o, ga, gg, w_branch_attn, w_branch_gm, w_out, norm2_g,
               router_w, router_b, w_gate_up, b_gate_up, w_down, b_down):
    merged = jax.nn.sigmoid(ga) * (attn_o @ w_branch_attn) + jax.nn.sigmoid(gg) * (gm_o @ w_branch_gm)
    h = x + merged @ w_out
    B, S, D = h.shape
    f = moe_ffn(rms_norm(h, norm2_g).reshape(B * S, D), router_w, router_b,
                w_gate_up, b_gate_up, w_down, b_down)
    return h + f.reshape(B, S, D)


def setup_inputs(seed: int = 0) -> dict:
    key = jax.random.key(seed)
    ks = jax.random.split(key, 24)
    f32 = jnp.float32

    def nrm(k, shape, scale):
        return jax.random.normal(k, shape, f32) * scale

    buf = min(WINDOW, PAST_LEN)
    return {
        "x_prompt": nrm(ks[0], (BATCH, SEQ, D_MODEL), 1.0),
        "x_sample": nrm(ks[1], (DEC_BATCH, DEC_SEQ, D_MODEL), 1.0),
        "cache_k": nrm(ks[2], (DEPTH, DEC_BATCH, buf, N_KV_HEADS, HEAD_DIM), 1.0),
        "cache_v": nrm(ks[3], (DEPTH, DEC_BATCH, buf, N_KV_HEADS, HEAD_DIM), 1.0),
        "norm1_g": 1.0 + nrm(ks[4], (DEPTH, D_MODEL), 0.02),
        "w_in": nrm(ks[5], (DEPTH, D_MODEL, IN_WIDTH), D_MODEL ** -0.5),
        "attn_sinks": nrm(ks[6], (DEPTH, N_HEADS), 1.0),
        "rel_bias": nrm(ks[7], (NUM_BUCKETS, N_HEADS), 0.5),
        "gm_ln_g": 1.0 + nrm(ks[8], (DEPTH, GM_WIDTH), 0.02),
        "gm_ln_b": nrm(ks[9], (DEPTH, GM_WIDTH), 0.02),
        "gm_ws": nrm(ks[10], (DEPTH, GM_GROUPS, CHUNK, CHUNK), CHUNK ** -0.5),
        "gm_bs": 1.0 + nrm(ks[11], (DEPTH, GM_GROUPS, CHUNK), 0.1),
        "w_branch_attn": nrm(ks[12], (DEPTH, ATTN_WIDTH, D_MODEL), ATTN_WIDTH ** -0.5),
        "w_branch_gm": nrm(ks[13], (DEPTH, GM_WIDTH, D_MODEL), GM_WIDTH ** -0.5),
        "w_out": nrm(ks[14], (DEPTH, D_MODEL, D_MODEL), D_MODEL ** -0.5),
        "norm2_g": 1.0 + nrm(ks[15], (DEPTH, D_MODEL), 0.02),
        "router_w": nrm(ks[16], (DEPTH, D_MODEL, N_EXPERTS), D_MODEL ** -0.5),
        "router_b": nrm(ks[17], (DEPTH, N_EXPERTS), 0.01),
        "w_gate_up": nrm(ks[18], (DEPTH, N_EXPERTS, D_MODEL, 2 * D_FF_EXPERT), D_MODEL ** -0.5),
        "b_gate_up": nrm(ks[19], (DEPTH, N_EXPERTS, 2 * D_FF_EXPERT), 0.01),
        "w_down": nrm(ks[20], (DEPTH, N_EXPERTS, D_FF_EXPERT, D_MODEL), D_FF_EXPERT ** -0.5),
        "b_down": nrm(ks[21], (DEPTH, N_EXPERTS, D_MODEL), 0.01),
        "final_g": 1.0 + nrm(ks[22], (D_MODEL,), 0.02),
    }


def reference(x_prompt, x_sample, cache_k, cache_v, norm1_g, w_in, attn_sinks, rel_bias,
              gm_ln_g, gm_ln_b, gm_ws, gm_bs, w_branch_attn, w_branch_gm, w_out, norm2_g,
              router_w, router_b, w_gate_up, b_gate_up, w_down, b_down, final_g):
    hp, hs = x_prompt, x_sample
    kp_rows, vp_rows, ks_rows, vs_rows, gv_rows = [], [], [], [], []
    for l in range(DEPTH):
        tail = (w_branch_attn[l], w_branch_gm[l], w_out[l], norm2_g[l], router_w[l], router_b[l],
                w_gate_up[l], b_gate_up[l], w_down[l], b_down[l])
        q, k, v, gu, gv, ga, gg = mixer_project(hp, norm1_g[l], w_in[l])
        attn_o = swa_prompt(q, k, v, attn_sinks[l], rel_bias)
        gm_o, _ = chunk_spatial_gating(gu, gv, gm_ln_g[l], gm_ln_b[l], gm_ws[l], gm_bs[l])
        keep = min(WINDOW, hp.shape[1])
        kp_rows.append(k[:, -keep:])
        vp_rows.append(v[:, -keep:])
        hp = block_tail(hp, attn_o, gm_o, ga, gg, *tail)
        q, k, v, gu, gv, ga, gg = mixer_project(hs, norm1_g[l], w_in[l])
        attn_o, k_buf, v_buf = swa_sample(q, k, v, cache_k[l], cache_v[l], attn_sinks[l], rel_bias)
        gm_o, vn = chunk_spatial_gating(gu, gv, gm_ln_g[l], gm_ln_b[l], gm_ws[l], gm_bs[l])
        ks_rows.append(k_buf)
        vs_rows.append(v_buf)
        gv_rows.append(vn)
        hs = block_tail(hs, attn_o, gm_o, ga, gg, *tail)
    y_prompt = rms_norm(hp, final_g)
    y_sample = rms_norm(hs, final_g)
    new_k_prompt = jnp.stack(kp_rows)
    new_v_prompt = jnp.stack(vp_rows)
    new_k_sample = jnp.stack(ks_rows)
    new_v_sample = jnp.stack(vs_rows)
    new_chunk_v_sample = jnp.stack(gv_rows)
    return (y_prompt, y_sample, new_k_prompt, new_v_prompt, new_k_sample, new_v_sample, new_chunk_v_sample)
```

```python
import functools
import math

import numpy as np
import jax
import jax.numpy as jnp
from jax import lax
from jax.experimental import pallas as pl
from jax.experimental.pallas import tpu as pltpu

F32 = jnp.float32
BF16 = jnp.bfloat16
U32 = jnp.uint32
I32 = jnp.int32

D_MODEL = 2048
HEAD_DIM = 64
N_HEADS = 16
N_KV_HEADS = 4
GQA_GROUP = N_HEADS // N_KV_HEADS
ATTN_WIDTH = N_HEADS * HEAD_DIM
KV_WIDTH = N_KV_HEADS * HEAD_DIM
WINDOW = 128
NUM_BUCKETS = 32
MAX_DISTANCE = 128
GM_WIDTH = D_MODEL // 2
GM_GROUPS = 8
CHUNK = 128
N_EXPERTS = 32
TOP_K = 4
D_FF = D_MODEL
SWIGLU_ALPHA = 1.702
SWIGLU_LIMIT = 7.0
NORM_EPS = 1e-5
NEG_INF = -1e30
IN_WIDTH = ATTN_WIDTH + 2 * KV_WIDTH + 2 * GM_WIDTH + 2 * D_MODEL

LANES = 128
IN_TM = 1024
IN_TN = 512
GM_ROWS = 512
TAIL_TM = 256
EXPERT_ROWS = 256
FF_CHUNK = 1024
SAMPLE_SEQS = 8
SAMPLE_KEYS = 256
PACKED = D_MODEL // 2
VMEM_LIMIT = 56 * 1024 * 1024


def _params(n_axes, vmem=None):
    return pltpu.CompilerParams(dimension_semantics=("arbitrary",) * n_axes, vmem_limit_bytes=vmem)


def _pack_pair(lo, hi):
    lo_b = lax.bitcast_convert_type(lo.astype(BF16).astype(F32), U32)
    hi_b = lax.bitcast_convert_type(hi.astype(BF16).astype(F32), U32)
    return (lo_b >> 16) | (hi_b & jnp.uint32(0xFFFF0000))


def _unpack_pair(w):
    lo = lax.bitcast_convert_type(w << 16, F32)
    hi = lax.bitcast_convert_type(w & jnp.uint32(0xFFFF0000), F32)
    return lo, hi


def _rms(x, g):
    return (x * lax.rsqrt(jnp.mean(x * x, axis=-1, keepdims=True) + NORM_EPS)) * g


def _inproj_body(x_ref, g_ref, w_ref, q_ref, kv_ref, u_ref, gv_ref, sga_ref, sgg_ref, xn_ref):
    j = pl.program_id(1)

    @pl.when(j == 0)
    def _():
        xn_ref[...] = _rms(x_ref[...], g_ref[...]).astype(BF16)

    acc = jnp.dot(xn_ref[...], w_ref[...], preferred_element_type=F32)

    @pl.when(j < 2)
    def _():
        q_ref[...] = (acc * (HEAD_DIM ** -0.5)).astype(q_ref.dtype)

    @pl.when(j == 2)
    def _():
        kv_ref[...] = acc

    @pl.when((j >= 3) & (j < 5))
    def _():
        u_ref[...] = jax.nn.gelu(acc).astype(BF16)

    @pl.when((j >= 5) & (j < 7))
    def _():
        gv_ref[...] = acc

    @pl.when((j >= 7) & (j < 11))
    def _():
        sga_ref[...] = jax.nn.sigmoid(acc).astype(BF16)

    @pl.when(j >= 11)
    def _():
        sgg_ref[...] = jax.nn.sigmoid(acc).astype(BF16)


def _in_project(x2d, g1, w_bf, q_dtype):
    t = x2d.shape[0]
    grid = (t // IN_TM, IN_WIDTH // IN_TN)

    def cols(first, n):
        return lambda i, j: (i, jnp.clip(j - first, 0, n - 1))

    blk = lambda first, n: pl.BlockSpec((IN_TM, IN_TN), cols(first, n))
    return pl.pallas_call(
        _inproj_body,
        grid=grid,
        in_specs=[
            pl.BlockSpec((IN_TM, D_MODEL), lambda i, j: (i, 0)),
            pl.BlockSpec((1, D_MODEL), lambda i, j: (0, 0)),
            pl.BlockSpec((D_MODEL, IN_TN), lambda i, j: (0, j)),
        ],
        out_specs=[blk(0, 2), blk(2, 1), blk(3, 2), blk(5, 2), blk(7, 4), blk(11, 4)],
        out_shape=[
            jax.ShapeDtypeStruct((t, ATTN_WIDTH), q_dtype),
            jax.ShapeDtypeStruct((t, 2 * KV_WIDTH), F32),
            jax.ShapeDtypeStruct((t, GM_WIDTH), BF16),
            jax.ShapeDtypeStruct((t, GM_WIDTH), F32),
            jax.ShapeDtypeStruct((t, D_MODEL), BF16),
            jax.ShapeDtypeStruct((t, D_MODEL), BF16),
        ],
        scratch_shapes=[pltpu.VMEM((IN_TM, D_MODEL), BF16)],
        compiler_params=_params(2, VMEM_LIMIT),
        name="in_project",
    )(x2d, g1, w_bf)


def _bucket_table(n_q, n_k, offset, n_valid_k):
    i = np.arange(n_q)[:, None]
    j = np.arange(n_k)[None, :]
    dist = offset + i - j
    n = np.maximum(dist, 0)
    max_exact = NUM_BUCKETS // 2
    nf = np.maximum(n, 1).astype(np.float64)
    large = max_exact + (np.log(nf / max_exact) / math.log(MAX_DISTANCE / max_exact)
                         * (NUM_BUCKETS - max_exact)).astype(np.int32)
    large = np.minimum(large, NUM_BUCKETS - 1)
    bucket = np.where(n < max_exact, n, large)
    valid = (dist >= 0) & (dist < WINDOW) & (j < n_valid_k)
    return jnp.asarray(np.where(valid, bucket, -1).astype(np.int32))


def _fill_bias(bkt_ref, rb_ref, bias_ref):
    bkt = bkt_ref[...]

    def per_head(h, carry):
        def per_bucket(b, acc):
            return jnp.where(bkt == b, rb_ref[b, h], acc)
        bias_ref[h] = lax.fori_loop(0, NUM_BUCKETS, per_bucket, jnp.full(bkt.shape, NEG_INF, F32))
        return carry

    lax.fori_loop(0, N_HEADS, per_head, 0)


def _sink_softmax(s, sink):
    m = jnp.maximum(jnp.max(s, axis=-1, keepdims=True), sink)
    p = jnp.exp(s - m)
    return p / (jnp.sum(p, axis=-1, keepdims=True) + jnp.exp(sink - m))


def _attn_prompt_body(q_ref, kvc_ref, kvp_ref, bkt_ref, rb_ref, sink_ref, o_ref, bias_ref, *, blocks_per_seq):
    s = pl.program_id(0)

    @pl.when(s == 0)
    def _():
        _fill_bias(bkt_ref, rb_ref, bias_ref)

    first = (s % blocks_per_seq) == 0
    kc = kvc_ref[...]
    kp = kvp_ref[...]
    k2 = jnp.concatenate([kp[:, :KV_WIDTH], kc[:, :KV_WIDTH]], axis=0).astype(BF16)
    v2 = jnp.concatenate([kp[:, KV_WIDTH:], kc[:, KV_WIDTH:]], axis=0).astype(BF16)
    col = lax.broadcasted_iota(I32, (WINDOW, 2 * WINDOW), 1)
    no_prev = (col < WINDOW) & first
    q = q_ref[...]
    for kh in range(N_KV_HEADS):
        k_h = k2[:, kh * HEAD_DIM:(kh + 1) * HEAD_DIM]
        v_h = v2[:, kh * HEAD_DIM:(kh + 1) * HEAD_DIM]
        heads = [kh * GQA_GROUP + g for g in range(GQA_GROUP)]
        qg = jnp.concatenate([q[:, h * HEAD_DIM:(h + 1) * HEAD_DIM] for h in heads], axis=0)
        sc = lax.dot_general(qg, k_h, (((1,), (1,)), ((), ())), preferred_element_type=F32)
        probs = []
        for g, h in enumerate(heads):
            bias = jnp.where(no_prev, NEG_INF, bias_ref[h])
            probs.append(_sink_softmax(sc[g * WINDOW:(g + 1) * WINDOW] + bias, sink_ref[h]).astype(BF16))
        o = jnp.dot(jnp.concatenate(probs, axis=0), v_h, preferred_element_type=F32)
        for g, h in enumerate(heads):
            o_ref[:, h * HEAD_DIM:(h + 1) * HEAD_DIM] = o[g * WINDOW:(g + 1) * WINDOW].astype(o_ref.dtype)


def _attention_prompt(q, kv, rel_bias, sinks, seq):
    t = q.shape[0]
    n_blocks = t // WINDOW
    bkt = _bucket_table(WINDOW, 2 * WINDOW, WINDOW, 2 * WINDOW)
    smem = pl.BlockSpec(memory_space=pltpu.SMEM)
    return pl.pallas_call(
        functools.partial(_attn_prompt_body, blocks_per_seq=seq // WINDOW),
        grid=(n_blocks,),
        in_specs=[
            pl.BlockSpec((WINDOW, ATTN_WIDTH), lambda s: (s, 0)),
            pl.BlockSpec((WINDOW, 2 * KV_WIDTH), lambda s: (s, 0)),
            pl.BlockSpec((WINDOW, 2 * KV_WIDTH), lambda s: (jnp.maximum(s - 1, 0), 0)),
            pl.BlockSpec((WINDOW, 2 * WINDOW), lambda s: (0, 0)),
            smem, smem,
        ],
        out_specs=pl.BlockSpec((WINDOW, ATTN_WIDTH), lambda s: (s, 0)),
        out_shape=jax.ShapeDtypeStruct((t, ATTN_WIDTH), BF16),
        scratch_shapes=[pltpu.VMEM((N_HEADS, WINDOW, 2 * WINDOW), F32)],
        compiler_params=_params(1),
        name="attention_prompt",
    )(q, kv, kv, bkt, rel_bias, sinks)


def _attn_sample_body(q_ref, k_ref, v_ref, bkt_ref, rb_ref, sink_ref, o_ref, bias_ref, *, n_q):
    @pl.when(pl.program_id(0) == 0)
    def _():
        _fill_bias(bkt_ref, rb_ref, bias_ref)

    k = k_ref[...].astype(BF16)
    v = v_ref[...].astype(BF16)
    q = q_ref[...]
    for kh in range(N_KV_HEADS):
        k_h = k[:, :, kh * HEAD_DIM:(kh + 1) * HEAD_DIM]
        v_h = v[:, :, kh * HEAD_DIM:(kh + 1) * HEAD_DIM]
        heads = [kh * GQA_GROUP + g for g in range(GQA_GROUP)]
        qg = jnp.concatenate([q[:, :, h * HEAD_DIM:(h + 1) * HEAD_DIM] for h in heads], axis=1).astype(BF16)
        sc = jnp.einsum('bqd,bsd->bqs', qg, k_h, preferred_element_type=F32)
        bias = jnp.concatenate([bias_ref[h] for h in heads], axis=0)
        sink = jnp.concatenate([jnp.full((n_q, 1), sink_ref[h], F32) for h in heads], axis=0)
        p = _sink_softmax(sc + bias[None], sink[None]).astype(BF16)
        o = jnp.einsum('bqs,bsd->bqd', p, v_h, preferred_element_type=F32)
        for g, h in enumerate(heads):
            o_ref[:, :, h * HEAD_DIM:(h + 1) * HEAD_DIM] = o[:, g * n_q:(g + 1) * n_q, :]


def _attention_sample(q3, kk, vv, rel_bias, sinks, n_cache):
    n_seq, n_q, _ = q3.shape
    bkt = _bucket_table(n_q, SAMPLE_KEYS, n_cache, n_cache + n_q)
    smem = pl.BlockSpec(memory_space=pltpu.SMEM)
    kv_spec = pl.BlockSpec((SAMPLE_SEQS, SAMPLE_KEYS, KV_WIDTH), lambda s: (s, 0, 0))
    q_spec = pl.BlockSpec((SAMPLE_SEQS, n_q, ATTN_WIDTH), lambda s: (s, 0, 0))
    return pl.pallas_call(
        functools.partial(_attn_sample_body, n_q=n_q),
        grid=(n_seq // SAMPLE_SEQS,),
        in_specs=[q_spec, kv_spec, kv_spec, pl.BlockSpec((n_q, SAMPLE_KEYS), lambda s: (0, 0)), smem, smem],
        out_specs=q_spec,
        out_shape=jax.ShapeDtypeStruct((n_seq, n_q, ATTN_WIDTH), F32),
        scratch_shapes=[pltpu.VMEM((N_HEADS, n_q, SAMPLE_KEYS), F32)],
        compiler_params=_params(1),
        name="attention_sample",
    )(q3, kk, vv, bkt, rel_bias, sinks)


def _gating_body(u_ref, gv_ref, lng_ref, lnb_ref, w_ref, bs_ref, *rest, seq_rows, emit_vn):
    if emit_vn:
        o_ref, vn_ref, wm_ref = rest
    else:
        o_ref, wm_ref = rest

    @pl.when(pl.program_id(0) == 0)
    def _():
        r = lax.broadcasted_iota(I32, (CHUNK, CHUNK), 0)
        c = lax.broadcasted_iota(I32, (CHUNK, CHUNK), 1)
        keep = c <= r
        if seq_rows < CHUNK:
            shift = seq_rows.bit_length() - 1
            assert seq_rows == 1 << shift
            keep = keep & ((r >> shift) == (c >> shift))
        for g in range(GM_GROUPS):
            wm_ref[g] = jnp.where(keep, w_ref[g], 0.0).astype(BF16)

    a = jax.nn.gelu(gv_ref[...])
    mu = jnp.mean(a, axis=-1, keepdims=True)
    var = jnp.mean(jnp.square(a - mu), axis=-1, keepdims=True)
    vn = ((a - mu) * lax.rsqrt(var + NORM_EPS)) * lng_ref[...] + lnb_ref[...]
    if emit_vn:
        vn_ref[...] = vn
    vnb = vn.astype(BF16)
    for ch in range(GM_ROWS // CHUNK):
        rows = slice(ch * CHUNK, (ch + 1) * CHUNK)
        for g in range(GM_GROUPS):
            cols = slice(g * CHUNK, (g + 1) * CHUNK)
            mixed = jnp.dot(wm_ref[g], vnb[rows, cols], preferred_element_type=F32) + bs_ref[g]
            o_ref[rows, cols] = (u_ref[rows, cols].astype(F32) * mixed).astype(BF16)


def _spatial_gating(u, gv, ln_g, ln_b, ws, bs_col, seq_rows, emit_vn):
    t = u.shape[0]
    row_spec = pl.BlockSpec((GM_ROWS, GM_WIDTH), lambda i: (i, 0))
    vec_spec = pl.BlockSpec((1, GM_WIDTH), lambda i: (0, 0))
    out_shape = [jax.ShapeDtypeStruct((t, GM_WIDTH), BF16)]
    out_specs = [row_spec]
    if emit_vn:
        out_shape.append(jax.ShapeDtypeStruct((t, GM_WIDTH), F32))
        out_specs.append(row_spec)
    return pl.pallas_call(
        functools.partial(_gating_body, seq_rows=seq_rows, emit_vn=emit_vn),
        grid=(t // GM_ROWS,),
        in_specs=[row_spec, row_spec, vec_spec, vec_spec,
                  pl.BlockSpec((GM_GROUPS, CHUNK, CHUNK), lambda i: (0, 0, 0)),
                  pl.BlockSpec((GM_GROUPS, CHUNK, 1), lambda i: (0, 0, 0))],
        out_specs=out_specs,
        out_shape=out_shape,
        scratch_shapes=[pltpu.VMEM((GM_GROUPS, CHUNK, CHUNK), BF16)],
        compiler_params=_params(1),
        name="spatial_gating",
    )(u, gv, ln_g, ln_b, ws, bs_col)


def _lanes4(vals, dtype):
    lane = lax.broadcasted_iota(I32, (vals[0].shape[0], LANES), 1)
    out = jnp.zeros((vals[0].shape[0], LANES), dtype)
    for k, v in enumerate(vals):
        out = jnp.where(lane == k, v.astype(dtype), out)
    return out


def _tail_body(x_ref, ao_ref, gm_ref, sga_ref, sgg_ref, wba_ref, wbg_ref, wout_ref, g2_ref, rw_ref, rb_ref,
               cin_ref, h_ref, hnp_ref, e_ref, gate_ref, rank_ref, cout_ref, carry_ref):
    @pl.when(pl.program_id(0) == 0)
    def _():
        carry_ref[...] = cin_ref[...].astype(F32)

    a = jnp.dot(ao_ref[...].astype(BF16), wba_ref[...], preferred_element_type=F32)
    m = jnp.dot(gm_ref[...], wbg_ref[...], preferred_element_type=F32)
    merged = sga_ref[...].astype(F32) * a + sgg_ref[...].astype(F32) * m
    h = x_ref[...] + jnp.dot(merged.astype(BF16), wout_ref[...], preferred_element_type=F32)
    h_ref[...] = h
    hn = _rms(h, g2_ref[...])
    hnp_ref[...] = _pack_pair(hn[:, :PACKED], hn[:, PACKED:])

    logits = jnp.dot(hn.astype(BF16), rw_ref[...], preferred_element_type=F32) + rb_ref[...]
    tm = logits.shape[0]
    lane = lax.broadcasted_iota(I32, (tm, N_EXPERTS), 1).astype(F32)
    work = logits
    tops, idxs, hots = [], [], []
    for _ in range(TOP_K):
        mx = jnp.max(work, axis=-1, keepdims=True)
        idx = jnp.min(jnp.where(work == mx, lane, float(N_EXPERTS)), axis=-1, keepdims=True)
        hot = lane == idx
        tops.append(mx)
        idxs.append(idx)
        hots.append(hot)
        work = jnp.where(hot, -jnp.inf, work)
    ex = [jnp.exp(v - tops[0]) for v in tops]
    den = ex[0] + ex[1] + ex[2] + ex[3]
    gates = [v / den for v in ex]

    onehot = sum(jnp.where(hot, 1.0, 0.0) for hot in hots)
    r = lax.broadcasted_iota(I32, (tm, tm), 0)
    c = lax.broadcasted_iota(I32, (tm, tm), 1)
    earlier = jnp.where(c < r, 1.0, 0.0).astype(BF16)
    prefix = jnp.dot(earlier, onehot.astype(BF16), preferred_element_type=F32) + carry_ref[...]
    ranks = [jnp.sum(jnp.where(hot, prefix, 0.0), axis=-1, keepdims=True) for hot in hots]
    carry_ref[...] = carry_ref[...] + jnp.sum(onehot, axis=0, keepdims=True)

    e_ref[...] = _lanes4(idxs, I32)
    gate_ref[...] = _lanes4(gates, F32)
    rank_ref[...] = _lanes4(ranks, I32)
    cout_ref[...] = carry_ref[...].astype(I32)


def _block_tail(x2d, attn_o, gm_o, sga, sgg, wba, wbg, wout, g2, rw, rb, counts_in):
    t = x2d.shape[0]
    tm = TAIL_TM
    row = lambda w: pl.BlockSpec((tm, w), lambda i: (i, 0))
    once = lambda shape: pl.BlockSpec(shape, lambda i: (0, 0), pipeline_mode=pl.Buffered(1))
    small = lambda shape: pl.BlockSpec(shape, lambda i: (0, 0))
    return pl.pallas_call(
        _tail_body,
        grid=(t // tm,),
        in_specs=[row(D_MODEL), row(ATTN_WIDTH), row(GM_WIDTH), row(D_MODEL), row(D_MODEL),
                  once((ATTN_WIDTH, D_MODEL)), once((GM_WIDTH, D_MODEL)), once((D_MODEL, D_MODEL)),
                  small((1, D_MODEL)), small((D_MODEL, N_EXPERTS)), small((1, N_EXPERTS)), small((1, N_EXPERTS))],
        out_specs=[row(D_MODEL), row(PACKED), row(LANES), row(LANES), row(LANES), small((1, N_EXPERTS))],
        out_shape=[
            jax.ShapeDtypeStruct((t, D_MODEL), F32),
            jax.ShapeDtypeStruct((t, PACKED), U32),
            jax.ShapeDtypeStruct((t, LANES), I32),
            jax.ShapeDtypeStruct((t, LANES), F32),
            jax.ShapeDtypeStruct((t, LANES), I32),
            jax.ShapeDtypeStruct((1, N_EXPERTS), I32),
        ],
        scratch_shapes=[pltpu.VMEM((1, N_EXPERTS), F32)],
        compiler_params=_params(1, VMEM_LIMIT),
        name="block_tail",
    )(x2d, attn_o, gm_o, sga, sgg, wba, wbg, wout, g2, rw, rb, counts_in)


_WAIT_UNROLL = 16


def _wait_rows(make_copy, n):
    def body(_, carry):
        for _ in range(_WAIT_UNROLL):
            make_copy().wait()
        return carry
    lax.fori_loop(0, n // _WAIT_UNROLL, body, 0)


def _dispatch_body(dest_ref, hn_ref, xs_in_ref, xs_ref, sem):
    del xs_in_ref
    tm = hn_ref.shape[0]

    def issue(t, carry):
        for k in range(TOP_K):
            d = dest_ref[t * TOP_K + k]
            pltpu.make_async_copy(hn_ref.at[pl.ds(t, 1), :], xs_ref.at[pl.ds(d, 1), :], sem).start()
        return carry

    lax.fori_loop(0, tm, issue, 0, unroll=4)
    _wait_rows(lambda: pltpu.make_async_copy(hn_ref.at[pl.ds(0, 1), :], xs_ref.at[pl.ds(0, 1), :], sem),
               tm * TOP_K)


def _dispatch(dest_flat, hnp, xs):
    t = hnp.shape[0]
    tm = TAIL_TM
    return pl.pallas_call(
        _dispatch_body,
        grid=(t // tm,),
        in_specs=[pl.BlockSpec((tm * TOP_K,), lambda i: (i,), memory_space=pltpu.SMEM),
                  pl.BlockSpec((tm, PACKED), lambda i: (i, 0)),
                  pl.BlockSpec(memory_space=pl.ANY)],
        out_specs=pl.BlockSpec(memory_space=pl.ANY),
        out_shape=jax.ShapeDtypeStruct(xs.shape, xs.dtype),
        scratch_shapes=[pltpu.SemaphoreType.DMA],
        input_output_aliases={2: 0},
        compiler_params=_params(1),
        name="dispatch",
    )(dest_flat, hnp, xs)


def _combine_body(dest_ref, gate_ref, h_ref, fg_ref, yb_ref, o_ref, buf_ref, sem):
    tm = h_ref.shape[0]

    def issue(t, carry):
        for k in range(TOP_K):
            d = dest_ref[t * TOP_K + k]
            pltpu.make_async_copy(yb_ref.at[pl.ds(d, 1), :], buf_ref.at[k, pl.ds(t, 1), :], sem).start()
        return carry

    lax.fori_loop(0, tm, issue, 0, unroll=4)
    _wait_rows(lambda: pltpu.make_async_copy(yb_ref.at[pl.ds(0, 1), :], buf_ref.at[0, pl.ds(0, 1), :], sem),
               tm * TOP_K)

    gate = gate_ref[...]
    f_lo = jnp.zeros((tm, PACKED), F32)
    f_hi = jnp.zeros((tm, PACKED), F32)
    for k in range(TOP_K):
        lo, hi = _unpack_pair(buf_ref[k])
        gk = gate[:, k:k + 1]
        f_lo = f_lo + gk * lo
        f_hi = f_hi + gk * hi
    y = h_ref[...] + jnp.concatenate([f_lo, f_hi], axis=1)
    o_ref[...] = _rms(y, fg_ref[...])


def _combine(dest_flat, gate, h, final_g, yb):
    t = h.shape[0]
    tm = TAIL_TM
    return pl.pallas_call(
        _combine_body,
        grid=(t // tm,),
        in_specs=[pl.BlockSpec((tm * TOP_K,), lambda i: (i,), memory_space=pltpu.SMEM),
                  pl.BlockSpec((tm, LANES), lambda i: (i, 0)),
                  pl.BlockSpec((tm, D_MODEL), lambda i: (i, 0)),
                  pl.BlockSpec((1, D_MODEL), lambda i: (0, 0)),
                  pl.BlockSpec(memory_space=pl.ANY)],
        out_specs=pl.BlockSpec((tm, D_MODEL), lambda i: (i, 0)),
        out_shape=jax.ShapeDtypeStruct((t, D_MODEL), F32),
        scratch_shapes=[pltpu.VMEM((TOP_K, tm, PACKED), U32), pltpu.SemaphoreType.DMA],
        compiler_params=_params(1),
        name="combine",
    )(dest_flat, gate, h, final_g, yb)


def _expert_changed(i, be_ref):
    return (i == 0) | (be_ref[i] != be_ref[jnp.maximum(i - 1, 0)])


def _gate_up_body(be_ref, nu_ref, xs_ref, wg_ref, wl_ref, bg_ref, bl_ref, o_ref, wgb_ref, wlb_ref):
    i = pl.program_id(1)
    live = i < nu_ref[0]

    @pl.when(live & _expert_changed(i, be_ref))
    def _():
        wgb_ref[...] = wg_ref[...].astype(BF16)
        wlb_ref[...] = wl_ref[...].astype(BF16)

    @pl.when(live)
    def _():
        lo, hi = _unpack_pair(xs_ref[...])
        x = jnp.concatenate([lo, hi], axis=1).astype(BF16)
        glu = jnp.dot(x, wgb_ref[...], preferred_element_type=F32) + bg_ref[...]
        lin = jnp.dot(x, wlb_ref[...], preferred_element_type=F32) + bl_ref[...]
        glu = jnp.minimum(glu, SWIGLU_LIMIT)
        lin = jnp.clip(lin, -SWIGLU_LIMIT, SWIGLU_LIMIT)
        o_ref[...] = (glu * jax.nn.sigmoid(SWIGLU_ALPHA * glu) * (lin + 1.0)).astype(BF16)

    @pl.when(jnp.logical_not(live))
    def _():
        o_ref[...] = jnp.zeros_like(o_ref)


def _gate_up(block_e, n_used, xs, w_gate_up, b_gate_up):
    n_blocks = xs.shape[0] // EXPERT_ROWS
    n_chunks = D_FF // FF_CHUNK
    blk = lambda i, nu: jnp.minimum(i, nu[0] - 1)
    grid_spec = pltpu.PrefetchScalarGridSpec(
        num_scalar_prefetch=2,
        grid=(n_chunks, n_blocks),
        in_specs=[
            pl.BlockSpec((EXPERT_ROWS, PACKED), lambda c, i, be, nu: (blk(i, nu), 0)),
            pl.BlockSpec((None, D_MODEL, FF_CHUNK), lambda c, i, be, nu: (be[blk(i, nu)], 0, c)),
            pl.BlockSpec((None, D_MODEL, FF_CHUNK), lambda c, i, be, nu: (be[blk(i, nu)], 0, n_chunks + c)),
            pl.BlockSpec((None, 1, FF_CHUNK), lambda c, i, be, nu: (be[blk(i, nu)], 0, c)),
            pl.BlockSpec((None, 1, FF_CHUNK), lambda c, i, be, nu: (be[blk(i, nu)], 0, n_chunks + c)),
        ],
        out_specs=pl.BlockSpec((EXPERT_ROWS, FF_CHUNK), lambda c, i, be, nu: (i, c)),
        scratch_shapes=[pltpu.VMEM((D_MODEL, FF_CHUNK), BF16), pltpu.VMEM((D_MODEL, FF_CHUNK), BF16)],
    )
    return pl.pallas_call(
        _gate_up_body,
        grid_spec=grid_spec,
        out_shape=jax.ShapeDtypeStruct((xs.shape[0], D_FF), BF16),
        compiler_params=_params(2, VMEM_LIMIT),
        name="expert_gate_up",
    )(block_e, n_used, xs, w_gate_up, w_gate_up, b_gate_up, b_gate_up)


def _down_body(be_ref, nu_ref, a_ref, w_ref, b_ref, o_ref, wb_ref):
    i = pl.program_id(0)
    live = i < nu_ref[0]

    @pl.when(live & _expert_changed(i, be_ref))
    def _():
        wb_ref[...] = w_ref[...].astype(BF16)

    @pl.when(live)
    def _():
        y = jnp.dot(a_ref[...], wb_ref[...], preferred_element_type=F32) + b_ref[...]
        o_ref[...] = _pack_pair(y[:, :PACKED], y[:, PACKED:])

    @pl.when(jnp.logical_not(live))
    def _():
        o_ref[...] = jnp.zeros_like(o_ref)


def _down(block_e, n_used, act, w_down, b_down):
    n_blocks = act.shape[0] // EXPERT_ROWS
    blk = lambda i, nu: jnp.minimum(i, nu[0] - 1)
    grid_spec = pltpu.PrefetchScalarGridSpec(
        num_scalar_prefetch=2,
        grid=(n_blocks,),
        in_specs=[
            pl.BlockSpec((EXPERT_ROWS, D_FF), lambda i, be, nu: (blk(i, nu), 0)),
            pl.BlockSpec((None, D_FF, D_MODEL), lambda i, be, nu: (be[blk(i, nu)], 0, 0)),
            pl.BlockSpec((None, 1, D_MODEL), lambda i, be, nu: (be[blk(i, nu)], 0, 0)),
        ],
        out_specs=pl.BlockSpec((EXPERT_ROWS, PACKED), lambda i, be, nu: (i, 0)),
        scratch_shapes=[pltpu.VMEM((D_FF, D_MODEL), BF16)],
    )
    return pl.pallas_call(
        _down_body,
        grid_spec=grid_spec,
        out_shape=jax.ShapeDtypeStruct((act.shape[0], PACKED), U32),
        compiler_params=_params(1, VMEM_LIMIT),
        name="expert_down",
    )(block_e, n_used, act, w_down, b_down)


def kernel(x_prompt, x_sample, cache_k, cache_v, norm1_g, w_in, attn_sinks, rel_bias, gm_ln_g, gm_ln_b, gm_ws,
           gm_bs, w_branch_attn, w_branch_gm, w_out, norm2_g, router_w, router_b, w_gate_up, b_gate_up, w_down,
           b_down, final_g):
    depth = w_in.shape[0]
    assert depth == 1, "single-layer stack"
    batch, seq, _ = x_prompt.shape
    dec_batch, dec_seq, _ = x_sample.shape
    n_cache = cache_k.shape[2]
    assert n_cache == WINDOW and seq % WINDOW == 0 and CHUNK % dec_seq == 0
    t_p, t_s = batch * seq, dec_batch * dec_seq
    l = 0

    w_in_bf = w_in[l].astype(BF16)
    wba, wbg, wout = w_branch_attn[l].astype(BF16), w_branch_gm[l].astype(BF16), w_out[l].astype(BF16)
    rw = router_w[l].astype(BF16)
    g1, g2, fg = norm1_g[l][None], norm2_g[l][None], final_g[None]
    rb = router_b[l][None]
    ln_g, ln_b = gm_ln_g[l][None], gm_ln_b[l][None]
    reps = CHUNK // dec_seq
    ws_p, bs_p = gm_ws[l], gm_bs[l][:, :, None]
    ws_s = jnp.tile(gm_ws[l][:, :dec_seq, :dec_seq], (1, reps, reps))
    bs_s = jnp.tile(gm_bs[l][:, :dec_seq], (1, reps))[:, :, None]
    b_gu = b_gate_up[l][:, None, :]
    b_dn = b_down[l][:, None, :]
    sinks = attn_sinks[l]

    xp = x_prompt.reshape(t_p, D_MODEL)
    xs_ = x_sample.reshape(t_s, D_MODEL)

    q_p, kv_p, u_p, gv_p, sga_p, sgg_p = _in_project(xp, g1, w_in_bf, BF16)
    ao_p = _attention_prompt(q_p, kv_p, rel_bias, sinks, seq)
    (gm_p,) = _spatial_gating(u_p, gv_p, ln_g, ln_b, ws_p, bs_p, CHUNK, False)

    q_s, kv_s, u_s, gv_s, sga_s, sgg_s = _in_project(xs_, g1, w_in_bf, F32)
    k_new = kv_s[:, :KV_WIDTH].reshape(dec_batch, dec_seq, KV_WIDTH)
    v_new = kv_s[:, KV_WIDTH:].reshape(dec_batch, dec_seq, KV_WIDTH)
    kk = jnp.concatenate([cache_k[l].reshape(dec_batch, n_cache, KV_WIDTH), k_new], axis=1)
    vv = jnp.concatenate([cache_v[l].reshape(dec_batch, n_cache, KV_WIDTH), v_new], axis=1)
    pad = ((0, 0), (0, SAMPLE_KEYS - n_cache - dec_seq), (0, 0))
    ao_s = _attention_sample(q_s.reshape(dec_batch, dec_seq, ATTN_WIDTH), jnp.pad(kk, pad), jnp.pad(vv, pad),
                             rel_bias, sinks, n_cache).reshape(t_s, ATTN_WIDTH)
    gm_s, vn_s = _spatial_gating(u_s, gv_s, ln_g, ln_b, ws_s, bs_s, dec_seq, True)

    zero_counts = jnp.zeros((1, N_EXPERTS), I32)
    h_p, hnp_p, e_p, gate_p, rank_p, counts_p = _block_tail(xp, ao_p, gm_p, sga_p, sgg_p, wba, wbg, wout, g2, rw, rb,
                                                            zero_counts)
    h_s, hnp_s, e_s, gate_s, rank_s, counts = _block_tail(xs_, ao_s, gm_s, sga_s, sgg_s, wba, wbg, wout, g2, rw, rb,
                                                          counts_p)

    counts = counts[0]
    padded = (counts + EXPERT_ROWS - 1) // EXPERT_ROWS * EXPERT_ROWS
    pad_end = jnp.cumsum(padded)
    pad_start = pad_end - padded
    n_blocks = (t_p + t_s) * TOP_K // EXPERT_ROWS + N_EXPERTS
    block_e = jnp.minimum(jnp.searchsorted(pad_end, jnp.arange(n_blocks, dtype=I32) * EXPERT_ROWS, side='right'),
                          N_EXPERTS - 1).astype(I32)
    n_used = (pad_end[-1:] // EXPERT_ROWS).astype(I32)
    dest_p = (pad_start[e_p[:, :TOP_K]] + rank_p[:, :TOP_K]).astype(I32).reshape(-1)
    dest_s = (pad_start[e_s[:, :TOP_K]] + rank_s[:, :TOP_K]).astype(I32).reshape(-1)

    xs_sorted = jnp.zeros((n_blocks * EXPERT_ROWS, PACKED), U32)
    xs_sorted = _dispatch(dest_p, hnp_p, xs_sorted)
    xs_sorted = _dispatch(dest_s, hnp_s, xs_sorted)
    act = _gate_up(block_e, n_used, xs_sorted, w_gate_up[l], b_gu)
    yb = _down(block_e, n_used, act, w_down[l], b_dn)
    y_p = _combine(dest_p, gate_p, h_p, fg, yb)
    y_s = _combine(dest_s, gate_s, h_s, fg, yb)

    keep = min(WINDOW, seq)
    k_p = kv_p[:, :KV_WIDTH].reshape(batch, seq, N_KV_HEADS, HEAD_DIM)[:, -keep:]
    v_p = kv_p[:, KV_WIDTH:].reshape(batch, seq, N_KV_HEADS, HEAD_DIM)[:, -keep:]
    k_s = kk[:, -n_cache:].reshape(dec_batch, n_cache, N_KV_HEADS, HEAD_DIM)
    v_s = vv[:, -n_cache:].reshape(dec_batch, n_cache, N_KV_HEADS, HEAD_DIM)
    return (y_p.reshape(batch, seq, D_MODEL), y_s.reshape(dec_batch, dec_seq, D_MODEL),
            k_p[None], v_p[None], k_s[None], v_s[None], vn_s.reshape(1, dec_batch, dec_seq, GM_WIDTH))
```

```python
import functools
import math

import numpy as np
import jax
import jax.numpy as jnp
from jax import lax
from jax.experimental import pallas as pl
from jax.experimental.pallas import tpu as pltpu

F32 = jnp.float32
BF16 = jnp.bfloat16
U32 = jnp.uint32
I32 = jnp.int32

D_MODEL = 2048
HEAD_DIM = 64
N_HEADS = 16
N_KV_HEADS = 4
GQA_GROUP = N_HEADS // N_KV_HEADS
ATTN_WIDTH = N_HEADS * HEAD_DIM
KV_WIDTH = N_KV_HEADS * HEAD_DIM
WINDOW = 128
NUM_BUCKETS = 32
MAX_DISTANCE = 128
GM_WIDTH = D_MODEL // 2
GM_GROUPS = 8
CHUNK = 128
N_EXPERTS = 32
TOP_K = 4
D_FF = D_MODEL
SWIGLU_ALPHA = 1.702
SWIGLU_LIMIT = 7.0
NORM_EPS = 1e-5
NEG_INF = -1e30
IN_WIDTH = ATTN_WIDTH + 2 * KV_WIDTH + 2 * GM_WIDTH + 2 * D_MODEL

LANES = 128
IN_TM = 1024
IN_TN = 512
ACT_TN = 1024
GM_ROWS = 512
TAIL_TM = 256
EXPERT_ROWS = 256
FF_CHUNK = 1024
SAMPLE_SEQS = 8
SAMPLE_KEYS = 256
PACKED = D_MODEL // 2
VMEM_LIMIT = 56 * 1024 * 1024

RAW_WIDTH = ATTN_WIDTH + GM_WIDTH + 2 * KV_WIDTH
RAW_Q_BLOCK = 0
RAW_GV_BLOCK = ATTN_WIDTH // GM_WIDTH
RAW_KV_BLOCK = (ATTN_WIDTH + GM_WIDTH) // (2 * KV_WIDTH)
RAW_K_COL = ATTN_WIDTH + GM_WIDTH


def _params(n_axes, vmem=None):
    return pltpu.CompilerParams(dimension_semantics=("arbitrary",) * n_axes, vmem_limit_bytes=vmem)


def _pack_pair(lo, hi):
    lo_b = lax.bitcast_convert_type(lo.astype(BF16).astype(F32), U32)
    hi_b = lax.bitcast_convert_type(hi.astype(BF16).astype(F32), U32)
    return (lo_b >> 16) | (hi_b & jnp.uint32(0xFFFF0000))


def _unpack_pair(w):
    lo = lax.bitcast_convert_type(w << 16, F32)
    hi = lax.bitcast_convert_type(w & jnp.uint32(0xFFFF0000), F32)
    return lo, hi


def _rms(x, g):
    return (x * lax.rsqrt(jnp.mean(x * x, axis=-1, keepdims=True) + NORM_EPS)) * g


def _proj_norm_body(x_ref, g_ref, w_ref, o_ref, xn_ref):
    @pl.when(pl.program_id(1) == 0)
    def _():
        xn_ref[...] = _rms(x_ref[...], g_ref[...]).astype(BF16)

    o_ref[...] = jnp.dot(xn_ref[...], w_ref[...], preferred_element_type=F32)


def _proj_body(xn_ref, w_ref, o_ref, *, epilogue):
    o_ref[...] = epilogue(jnp.dot(xn_ref[...], w_ref[...], preferred_element_type=F32)).astype(o_ref.dtype)


def _split_w_in(w_in):
    q_end = ATTN_WIDTH
    kv_end = q_end + 2 * KV_WIDTH
    gu_end = kv_end + GM_WIDTH
    gv_end = gu_end + GM_WIDTH
    w_raw = jnp.concatenate([w_in[:, :q_end], w_in[:, gu_end:gv_end], w_in[:, q_end:kv_end]], axis=1)
    return w_raw.astype(BF16), w_in[:, kv_end:gu_end].astype(BF16), w_in[:, gv_end:].astype(BF16)


def _in_project(x2d, g1, w_raw, w_gelu, w_sig):
    t = x2d.shape[0]

    raw, xn = pl.pallas_call(
        _proj_norm_body,
        grid=(t // IN_TM, RAW_WIDTH // IN_TN),
        in_specs=[pl.BlockSpec((IN_TM, D_MODEL), lambda i, j: (i, 0)),
                  pl.BlockSpec((1, D_MODEL), lambda i, j: (0, 0)),
                  pl.BlockSpec((D_MODEL, IN_TN), lambda i, j: (0, j))],
        out_specs=[pl.BlockSpec((IN_TM, IN_TN), lambda i, j: (i, j)),
                   pl.BlockSpec((IN_TM, D_MODEL), lambda i, j: (i, 0))],
        out_shape=[jax.ShapeDtypeStruct((t, RAW_WIDTH), F32), jax.ShapeDtypeStruct((t, D_MODEL), BF16)],
        compiler_params=_params(2, VMEM_LIMIT),
        name="in_project_raw",
    )(x2d, g1, w_raw)

    def act_call(w, epilogue, name):
        width = w.shape[1]
        return pl.pallas_call(
            functools.partial(_proj_body, epilogue=epilogue),
            grid=(t // IN_TM, width // ACT_TN),
            in_specs=[pl.BlockSpec((IN_TM, D_MODEL), lambda i, j: (i, 0)),
                      pl.BlockSpec((D_MODEL, ACT_TN), lambda i, j: (0, j))],
            out_specs=pl.BlockSpec((IN_TM, ACT_TN), lambda i, j: (i, j)),
            out_shape=jax.ShapeDtypeStruct((t, width), BF16),
            compiler_params=_params(2, VMEM_LIMIT),
            name=name,
        )(xn, w)

    return raw, act_call(w_gelu, jax.nn.gelu, "in_project_gelu"), act_call(w_sig, jax.nn.sigmoid, "in_project_sigmoid")


def _bucket_table(n_q, n_k, offset, n_valid_k):
    i = np.arange(n_q)[:, None]
    j = np.arange(n_k)[None, :]
    dist = offset + i - j
    n = np.maximum(dist, 0)
    max_exact = NUM_BUCKETS // 2
    nf = np.maximum(n, 1).astype(np.float64)
    large = max_exact + (np.log(nf / max_exact) / math.log(MAX_DISTANCE / max_exact)
                         * (NUM_BUCKETS - max_exact)).astype(np.int32)
    large = np.minimum(large, NUM_BUCKETS - 1)
    bucket = np.where(n < max_exact, n, large)
    valid = (dist >= 0) & (dist < WINDOW) & (j < n_valid_k)
    return jnp.asarray(np.where(valid, bucket, -1).astype(np.int32))


def _fill_bias(bkt_ref, rb_ref, bias_ref):
    bkt = bkt_ref[...]

    def per_head(h, carry):
        def per_bucket(b, acc):
            return jnp.where(bkt == b, rb_ref[b, h], acc)
        bias_ref[h] = lax.fori_loop(0, NUM_BUCKETS, per_bucket, jnp.full(bkt.shape, NEG_INF, F32))
        return carry

    lax.fori_loop(0, N_HEADS, per_head, 0)


def _sink_softmax(s, sink):
    m = jnp.maximum(jnp.max(s, axis=-1, keepdims=True), sink)
    p = jnp.exp(s - m)
    return p / (jnp.sum(p, axis=-1, keepdims=True) + jnp.exp(sink - m))


def _attn_prompt_body(q_ref, kvc_ref, kvp_ref, bkt_ref, rb_ref, sink_ref, o_ref, bias_ref, *, blocks_per_seq):
    s = pl.program_id(0)

    @pl.when(s == 0)
    def _():
        _fill_bias(bkt_ref, rb_ref, bias_ref)

    first = (s % blocks_per_seq) == 0
    kc = kvc_ref[...]
    kp = kvp_ref[...]
    k2 = jnp.concatenate([kp[:, :KV_WIDTH], kc[:, :KV_WIDTH]], axis=0).astype(BF16)
    v2 = jnp.concatenate([kp[:, KV_WIDTH:], kc[:, KV_WIDTH:]], axis=0).astype(BF16)
    col = lax.broadcasted_iota(I32, (WINDOW, 2 * WINDOW), 1)
    no_prev = (col < WINDOW) & first
    q = (q_ref[...] * (HEAD_DIM ** -0.5)).astype(BF16)
    for kh in range(N_KV_HEADS):
        k_h = k2[:, kh * HEAD_DIM:(kh + 1) * HEAD_DIM]
        v_h = v2[:, kh * HEAD_DIM:(kh + 1) * HEAD_DIM]
        heads = [kh * GQA_GROUP + g for g in range(GQA_GROUP)]
        qg = jnp.concatenate([q[:, h * HEAD_DIM:(h + 1) * HEAD_DIM] for h in heads], axis=0)
        sc = lax.dot_general(qg, k_h, (((1,), (1,)), ((), ())), preferred_element_type=F32)
        probs = []
        for g, h in enumerate(heads):
            bias = jnp.where(no_prev, NEG_INF, bias_ref[h])
            probs.append(_sink_softmax(sc[g * WINDOW:(g + 1) * WINDOW] + bias, sink_ref[h]).astype(BF16))
        o = jnp.dot(jnp.concatenate(probs, axis=0), v_h, preferred_element_type=F32)
        for g, h in enumerate(heads):
            o_ref[:, h * HEAD_DIM:(h + 1) * HEAD_DIM] = o[g * WINDOW:(g + 1) * WINDOW].astype(o_ref.dtype)


def _attention_prompt(raw, rel_bias, sinks, seq):
    t = raw.shape[0]
    n_blocks = t // WINDOW
    bkt = _bucket_table(WINDOW, 2 * WINDOW, WINDOW, 2 * WINDOW)
    smem = pl.BlockSpec(memory_space=pltpu.SMEM)
    return pl.pallas_call(
        functools.partial(_attn_prompt_body, blocks_per_seq=seq // WINDOW),
        grid=(n_blocks,),
        in_specs=[
            pl.BlockSpec((WINDOW, ATTN_WIDTH), lambda s: (s, RAW_Q_BLOCK)),
            pl.BlockSpec((WINDOW, 2 * KV_WIDTH), lambda s: (s, RAW_KV_BLOCK)),
            pl.BlockSpec((WINDOW, 2 * KV_WIDTH), lambda s: (jnp.maximum(s - 1, 0), RAW_KV_BLOCK)),
            pl.BlockSpec((WINDOW, 2 * WINDOW), lambda s: (0, 0)),
            smem, smem,
        ],
        out_specs=pl.BlockSpec((WINDOW, ATTN_WIDTH), lambda s: (s, 0)),
        out_shape=jax.ShapeDtypeStruct((t, ATTN_WIDTH), BF16),
        scratch_shapes=[pltpu.VMEM((N_HEADS, WINDOW, 2 * WINDOW), F32)],
        compiler_params=_params(1),
        name="attention_prompt",
    )(raw, raw, raw, bkt, rel_bias, sinks)


def _attn_sample_body(q_ref, kvn_ref, ck_ref, cv_ref, bkt_ref, rb_ref, sink_ref, o_ref, nk_ref, nv_ref, bias_ref,
                      *, n_q, n_cache):
    @pl.when(pl.program_id(0) == 0)
    def _():
        _fill_bias(bkt_ref, rb_ref, bias_ref)

    kvn = kvn_ref[...]
    zeros = jnp.zeros((SAMPLE_SEQS, SAMPLE_KEYS - n_cache - n_q, KV_WIDTH), F32)
    k_all = jnp.concatenate([ck_ref[...], kvn[:, :, :KV_WIDTH], zeros], axis=1)
    v_all = jnp.concatenate([cv_ref[...], kvn[:, :, KV_WIDTH:], zeros], axis=1)
    nk_ref[...] = k_all[:, n_q:n_q + n_cache]
    nv_ref[...] = v_all[:, n_q:n_q + n_cache]
    k = k_all.astype(BF16)
    v = v_all.astype(BF16)
    q = q_ref[...] * (HEAD_DIM ** -0.5)
    for kh in range(N_KV_HEADS):
        k_h = k[:, :, kh * HEAD_DIM:(kh + 1) * HEAD_DIM]
        v_h = v[:, :, kh * HEAD_DIM:(kh + 1) * HEAD_DIM]
        heads = [kh * GQA_GROUP + g for g in range(GQA_GROUP)]
        qg = jnp.concatenate([q[:, :, h * HEAD_DIM:(h + 1) * HEAD_DIM] for h in heads], axis=1).astype(BF16)
        sc = jnp.einsum('bqd,bsd->bqs', qg, k_h, preferred_element_type=F32)
        bias = jnp.concatenate([bias_ref[h] for h in heads], axis=0)
        sink = jnp.concatenate([jnp.full((n_q, 1), sink_ref[h], F32) for h in heads], axis=0)
        p = _sink_softmax(sc + bias[None], sink[None]).astype(BF16)
        o = jnp.einsum('bqs,bsd->bqd', p, v_h, preferred_element_type=F32)
        for g, h in enumerate(heads):
            o_ref[:, :, h * HEAD_DIM:(h + 1) * HEAD_DIM] = o[:, g * n_q:(g + 1) * n_q, :]


def _attention_sample(raw3, cache_k, cache_v, rel_bias, sinks):
    n_seq, n_q, _ = raw3.shape
    n_cache = cache_k.shape[1]
    bkt = _bucket_table(n_q, SAMPLE_KEYS, n_cache, n_cache + n_q)
    smem = pl.BlockSpec(memory_space=pltpu.SMEM)
    cache_spec = pl.BlockSpec((SAMPLE_SEQS, n_cache, KV_WIDTH), lambda s: (s, 0, 0))
    o_spec = pl.BlockSpec((SAMPLE_SEQS, n_q, ATTN_WIDTH), lambda s: (s, 0, 0))
    return pl.pallas_call(
        functools.partial(_attn_sample_body, n_q=n_q, n_cache=n_cache),
        grid=(n_seq // SAMPLE_SEQS,),
        in_specs=[pl.BlockSpec((SAMPLE_SEQS, n_q, ATTN_WIDTH), lambda s: (s, 0, RAW_Q_BLOCK)),
                  pl.BlockSpec((SAMPLE_SEQS, n_q, 2 * KV_WIDTH), lambda s: (s, 0, RAW_KV_BLOCK)),
                  cache_spec, cache_spec,
                  pl.BlockSpec((n_q, SAMPLE_KEYS), lambda s: (0, 0)), smem, smem],
        out_specs=[o_spec, cache_spec, cache_spec],
        out_shape=[jax.ShapeDtypeStruct((n_seq, n_q, ATTN_WIDTH), F32),
                   jax.ShapeDtypeStruct(cache_k.shape, F32), jax.ShapeDtypeStruct(cache_v.shape, F32)],
        scratch_shapes=[pltpu.VMEM((N_HEADS, n_q, SAMPLE_KEYS), F32)],
        compiler_params=_params(1),
        name="attention_sample",
    )(raw3, raw3, cache_k, cache_v, bkt, rel_bias, sinks)


def _gating_body(u_ref, gv_ref, lng_ref, lnb_ref, w_ref, bs_ref, *rest, seq_rows, emit_vn):
    if emit_vn:
        o_ref, vn_ref, wm_ref = rest
    else:
        o_ref, wm_ref = rest

    @pl.when(pl.program_id(0) == 0)
    def _():
        r = lax.broadcasted_iota(I32, (CHUNK, CHUNK), 0)
        c = lax.broadcasted_iota(I32, (CHUNK, CHUNK), 1)
        keep = c <= r
        if seq_rows < CHUNK:
            shift = seq_rows.bit_length() - 1
            assert seq_rows == 1 << shift
            keep = keep & ((r >> shift) == (c >> shift))
        for g in range(GM_GROUPS):
            wm_ref[g] = jnp.where(keep, w_ref[g], 0.0).astype(BF16)

    a = jax.nn.gelu(gv_ref[...])
    mu = jnp.mean(a, axis=-1, keepdims=True)
    var = jnp.mean(jnp.square(a - mu), axis=-1, keepdims=True)
    vn = ((a - mu) * lax.rsqrt(var + NORM_EPS)) * lng_ref[...] + lnb_ref[...]
    if emit_vn:
        vn_ref[...] = vn
    vnb = vn.astype(BF16)
    for ch in range(GM_ROWS // CHUNK):
        rows = slice(ch * CHUNK, (ch + 1) * CHUNK)
        for g in range(GM_GROUPS):
            cols = slice(g * CHUNK, (g + 1) * CHUNK)
            mixed = jnp.dot(wm_ref[g], vnb[rows, cols], preferred_element_type=F32) + bs_ref[g]
            o_ref[rows, cols] = (u_ref[rows, cols].astype(F32) * mixed).astype(BF16)


def _spatial_gating(u, raw, ln_g, ln_b, ws, bs_col, seq_rows, emit_vn):
    t = u.shape[0]
    row_spec = pl.BlockSpec((GM_ROWS, GM_WIDTH), lambda i: (i, 0))
    vec_spec = pl.BlockSpec((1, GM_WIDTH), lambda i: (0, 0))
    out_shape = [jax.ShapeDtypeStruct((t, GM_WIDTH), BF16)]
    out_specs = [row_spec]
    if emit_vn:
        out_shape.append(jax.ShapeDtypeStruct((t, GM_WIDTH), F32))
        out_specs.append(row_spec)
    return pl.pallas_call(
        functools.partial(_gating_body, seq_rows=seq_rows, emit_vn=emit_vn),
        grid=(t // GM_ROWS,),
        in_specs=[row_spec, pl.BlockSpec((GM_ROWS, GM_WIDTH), lambda i: (i, RAW_GV_BLOCK)), vec_spec, vec_spec,
                  pl.BlockSpec((GM_GROUPS, CHUNK, CHUNK), lambda i: (0, 0, 0)),
                  pl.BlockSpec((GM_GROUPS, CHUNK, 1), lambda i: (0, 0, 0))],
        out_specs=out_specs,
        out_shape=out_shape,
        scratch_shapes=[pltpu.VMEM((GM_GROUPS, CHUNK, CHUNK), BF16)],
        compiler_params=_params(1),
        name="spatial_gating",
    )(u, raw, ln_g, ln_b, ws, bs_col)


def _lanes4(vals, dtype):
    lane = lax.broadcasted_iota(I32, (vals[0].shape[0], LANES), 1)
    out = jnp.zeros((vals[0].shape[0], LANES), dtype)
    for k, v in enumerate(vals):
        out = jnp.where(lane == k, v.astype(dtype), out)
    return out


def _tail_body(x_ref, ao_ref, gm_ref, sga_ref, sgg_ref, wba_ref, wbg_ref, wout_ref, g2_ref, rw_ref, rb_ref,
               cin_ref, h_ref, hnp_ref, e_ref, gate_ref, rank_ref, cout_ref, carry_ref):
    @pl.when(pl.program_id(0) == 0)
    def _():
        carry_ref[...] = cin_ref[...].astype(F32)

    a = jnp.dot(ao_ref[...].astype(BF16), wba_ref[...], preferred_element_type=F32)
    m = jnp.dot(gm_ref[...], wbg_ref[...], preferred_element_type=F32)
    merged = sga_ref[...].astype(F32) * a + sgg_ref[...].astype(F32) * m
    h = x_ref[...] + jnp.dot(merged.astype(BF16), wout_ref[...], preferred_element_type=F32)
    h_ref[...] = h
    hn = _rms(h, g2_ref[...])
    hnp_ref[...] = _pack_pair(hn[:, :PACKED], hn[:, PACKED:])

    logits = jnp.dot(hn.astype(BF16), rw_ref[...], preferred_element_type=F32) + rb_ref[...]
    tm = logits.shape[0]
    lane = lax.broadcasted_iota(I32, (tm, N_EXPERTS), 1).astype(F32)
    work = logits
    tops, idxs, hots = [], [], []
    for _ in range(TOP_K):
        mx = jnp.max(work, axis=-1, keepdims=True)
        idx = jnp.min(jnp.where(work == mx, lane, float(N_EXPERTS)), axis=-1, keepdims=True)
        hot = lane == idx
        tops.append(mx)
        idxs.append(idx)
        hots.append(hot)
        work = jnp.where(hot, -jnp.inf, work)
    ex = [jnp.exp(v - tops[0]) for v in tops]
    den = ex[0] + ex[1] + ex[2] + ex[3]
    gates = [v / den for v in ex]

    onehot = sum(jnp.where(hot, 1.0, 0.0) for hot in hots)
    r = lax.broadcasted_iota(I32, (tm, tm), 0)
    c = lax.broadcasted_iota(I32, (tm, tm), 1)
    earlier = jnp.where(c < r, 1.0, 0.0).astype(BF16)
    prefix = jnp.dot(earlier, onehot.astype(BF16), preferred_element_type=F32) + carry_ref[...]
    ranks = [jnp.sum(jnp.where(hot, prefix, 0.0), axis=-1, keepdims=True) for hot in hots]
    carry_ref[...] = carry_ref[...] + jnp.sum(onehot, axis=0, keepdims=True)

    e_ref[...] = _lanes4(idxs, I32)
    gate_ref[...] = _lanes4(gates, F32)
    rank_ref[...] = _lanes4(ranks, I32)
    cout_ref[...] = carry_ref[...].astype(I32)


def _block_tail(x2d, attn_o, gm_o, gates, wba, wbg, wout, g2, rw, rb, counts_in):
    t = x2d.shape[0]
    tm = TAIL_TM
    row = lambda w, col=0: pl.BlockSpec((tm, w), lambda i: (i, col))
    once = lambda shape: pl.BlockSpec(shape, lambda i: (0, 0), pipeline_mode=pl.Buffered(1))
    small = lambda shape: pl.BlockSpec(shape, lambda i: (0, 0))
    return pl.pallas_call(
        _tail_body,
        grid=(t // tm,),
        in_specs=[row(D_MODEL), row(ATTN_WIDTH), row(GM_WIDTH), row(D_MODEL, 0), row(D_MODEL, 1),
                  once((ATTN_WIDTH, D_MODEL)), once((GM_WIDTH, D_MODEL)), once((D_MODEL, D_MODEL)),
                  small((1, D_MODEL)), small((D_MODEL, N_EXPERTS)), small((1, N_EXPERTS)), small((1, N_EXPERTS))],
        out_specs=[row(D_MODEL), row(PACKED), row(LANES), row(LANES), row(LANES), small((1, N_EXPERTS))],
        out_shape=[
            jax.ShapeDtypeStruct((t, D_MODEL), F32),
            jax.ShapeDtypeStruct((t, PACKED), U32),
            jax.ShapeDtypeStruct((t, LANES), I32),
            jax.ShapeDtypeStruct((t, LANES), F32),
            jax.ShapeDtypeStruct((t, LANES), I32),
            jax.ShapeDtypeStruct((1, N_EXPERTS), I32),
        ],
        scratch_shapes=[pltpu.VMEM((1, N_EXPERTS), F32)],
        compiler_params=_params(1, VMEM_LIMIT),
        name="block_tail",
    )(x2d, attn_o, gm_o, gates, gates, wba, wbg, wout, g2, rw, rb, counts_in)


_WAIT_UNROLL = 16


def _wait_rows(make_copy, n):
    def body(_, carry):
        for _ in range(_WAIT_UNROLL):
            make_copy().wait()
        return carry
    lax.fori_loop(0, n // _WAIT_UNROLL, body, 0)


def _dispatch_body(dest_ref, hn_ref, xs_in_ref, xs_ref, sem):
    del xs_in_ref
    tm = hn_ref.shape[0]

    def issue(t, carry):
        for k in range(TOP_K):
            d = dest_ref[t * TOP_K + k]
            pltpu.make_async_copy(hn_ref.at[pl.ds(t, 1), :], xs_ref.at[pl.ds(d, 1), :], sem).start()
        return carry

    lax.fori_loop(0, tm, issue, 0, unroll=4)
    _wait_rows(lambda: pltpu.make_async_copy(hn_ref.at[pl.ds(0, 1), :], xs_ref.at[pl.ds(0, 1), :], sem),
               tm * TOP_K)


def _dispatch(dest_flat, hnp, xs):
    t = hnp.shape[0]
    tm = TAIL_TM
    return pl.pallas_call(
        _dispatch_body,
        grid=(t // tm,),
        in_specs=[pl.BlockSpec((tm * TOP_K,), lambda i: (i,), memory_space=pltpu.SMEM),
                  pl.BlockSpec((tm, PACKED), lambda i: (i, 0)),
                  pl.BlockSpec(memory_space=pl.ANY)],
        out_specs=pl.BlockSpec(memory_space=pl.ANY),
        out_shape=jax.ShapeDtypeStruct(xs.shape, xs.dtype),
        scratch_shapes=[pltpu.SemaphoreType.DMA],
        input_output_aliases={2: 0},
        compiler_params=_params(1),
        name="dispatch",
    )(dest_flat, hnp, xs)


def _combine_body(dest_ref, gate_ref, h_ref, fg_ref, yb_ref, o_ref, buf_ref, sem):
    tm = h_ref.shape[0]

    def issue(t, carry):
        for k in range(TOP_K):
            d = dest_ref[t * TOP_K + k]
            pltpu.make_async_copy(yb_ref.at[pl.ds(d, 1), :], buf_ref.at[k, pl.ds(t, 1), :], sem).start()
        return carry

    lax.fori_loop(0, tm, issue, 0, unroll=4)
    _wait_rows(lambda: pltpu.make_async_copy(yb_ref.at[pl.ds(0, 1), :], buf_ref.at[0, pl.ds(0, 1), :], sem),
               tm * TOP_K)

    gate = gate_ref[...]
    f_lo = jnp.zeros((tm, PACKED), F32)
    f_hi = jnp.zeros((tm, PACKED), F32)
    for k in range(TOP_K):
        lo, hi = _unpack_pair(buf_ref[k])
        gk = gate[:, k:k + 1]
        f_lo = f_lo + gk * lo
        f_hi = f_hi + gk * hi
    y = h_ref[...] + jnp.concatenate([f_lo, f_hi], axis=1)
    o_ref[...] = _rms(y, fg_ref[...])


def _combine(dest_flat, gate, h, final_g, yb):
    t = h.shape[0]
    tm = TAIL_TM
    return pl.pallas_call(
        _combine_body,
        grid=(t // tm,),
        in_specs=[pl.BlockSpec((tm * TOP_K,), lambda i: (i,), memory_space=pltpu.SMEM),
                  pl.BlockSpec((tm, LANES), lambda i: (i, 0)),
                  pl.BlockSpec((tm, D_MODEL), lambda i: (i, 0)),
                  pl.BlockSpec((1, D_MODEL), lambda i: (0, 0)),
                  pl.BlockSpec(memory_space=pl.ANY)],
        out_specs=pl.BlockSpec((tm, D_MODEL), lambda i: (i, 0)),
        out_shape=jax.ShapeDtypeStruct((t, D_MODEL), F32),
        scratch_shapes=[pltpu.VMEM((TOP_K, tm, PACKED), U32), pltpu.SemaphoreType.DMA],
        compiler_params=_params(1),
        name="combine",
    )(dest_flat, gate, h, final_g, yb)


def _expert_changed(i, be_ref):
    return (i == 0) | (be_ref[i] != be_ref[jnp.maximum(i - 1, 0)])


def _gate_up_body(be_ref, nu_ref, xs_ref, wg_ref, wl_ref, bg_ref, bl_ref, o_ref, wgb_ref, wlb_ref):
    i = pl.program_id(1)
    live = i < nu_ref[0]

    @pl.when(live & _expert_changed(i, be_ref))
    def _():
        wgb_ref[...] = wg_ref[...].astype(BF16)
        wlb_ref[...] = wl_ref[...].astype(BF16)

    @pl.when(live)
    def _():
        lo, hi = _unpack_pair(xs_ref[...])
        x = jnp.concatenate([lo, hi], axis=1).astype(BF16)
        glu = jnp.dot(x, wgb_ref[...], preferred_element_type=F32) + bg_ref[...]
        lin = jnp.dot(x, wlb_ref[...], preferred_element_type=F32) + bl_ref[...]
        glu = jnp.minimum(glu, SWIGLU_LIMIT)
        lin = jnp.clip(lin, -SWIGLU_LIMIT, SWIGLU_LIMIT)
        o_ref[...] = (glu * jax.nn.sigmoid(SWIGLU_ALPHA * glu) * (lin + 1.0)).astype(BF16)

    @pl.when(jnp.logical_not(live))
    def _():
        o_ref[...] = jnp.zeros_like(o_ref)


def _gate_up(block_e, n_used, xs, w_gate_up, b_gate_up):
    n_blocks = xs.shape[0] // EXPERT_ROWS
    n_chunks = D_FF // FF_CHUNK
    blk = lambda i, nu: jnp.minimum(i, nu[0] - 1)
    grid_spec = pltpu.PrefetchScalarGridSpec(
        num_scalar_prefetch=2,
        grid=(n_chunks, n_blocks),
        in_specs=[
            pl.BlockSpec((EXPERT_ROWS, PACKED), lambda c, i, be, nu: (blk(i, nu), 0)),
            pl.BlockSpec((None, D_MODEL, FF_CHUNK), lambda c, i, be, nu: (be[blk(i, nu)], 0, c)),
            pl.BlockSpec((None, D_MODEL, FF_CHUNK), lambda c, i, be, nu: (be[blk(i, nu)], 0, n_chunks + c)),
            pl.BlockSpec((None, 1, FF_CHUNK), lambda c, i, be, nu: (be[blk(i, nu)], 0, c)),
            pl.BlockSpec((None, 1, FF_CHUNK), lambda c, i, be, nu: (be[blk(i, nu)], 0, n_chunks + c)),
        ],
        out_specs=pl.BlockSpec((EXPERT_ROWS, FF_CHUNK), lambda c, i, be, nu: (i, c)),
        scratch_shapes=[pltpu.VMEM((D_MODEL, FF_CHUNK), BF16), pltpu.VMEM((D_MODEL, FF_CHUNK), BF16)],
    )
    return pl.pallas_call(
        _gate_up_body,
        grid_spec=grid_spec,
        out_shape=jax.ShapeDtypeStruct((xs.shape[0], D_FF), BF16),
        compiler_params=_params(2, VMEM_LIMIT),
        name="expert_gate_up",
    )(block_e, n_used, xs, w_gate_up, w_gate_up, b_gate_up, b_gate_up)


def _down_body(be_ref, nu_ref, a_ref, w_ref, b_ref, o_ref, wb_ref):
    i = pl.program_id(0)
    live = i < nu_ref[0]

    @pl.when(live & _expert_changed(i, be_ref))
    def _():
        wb_ref[...] = w_ref[...].astype(BF16)

    @pl.when(live)
    def _():
        y = jnp.dot(a_ref[...], wb_ref[...], preferred_element_type=F32) + b_ref[...]
        o_ref[...] = _pack_pair(y[:, :PACKED], y[:, PACKED:])

    @pl.when(jnp.logical_not(live))
    def _():
        o_ref[...] = jnp.zeros_like(o_ref)


def _down(block_e, n_used, act, w_down, b_down):
    n_blocks = act.shape[0] // EXPERT_ROWS
    blk = lambda i, nu: jnp.minimum(i, nu[0] - 1)
    grid_spec = pltpu.PrefetchScalarGridSpec(
        num_scalar_prefetch=2,
        grid=(n_blocks,),
        in_specs=[
            pl.BlockSpec((EXPERT_ROWS, D_FF), lambda i, be, nu: (blk(i, nu), 0)),
            pl.BlockSpec((None, D_FF, D_MODEL), lambda i, be, nu: (be[blk(i, nu)], 0, 0)),
            pl.BlockSpec((None, 1, D_MODEL), lambda i, be, nu: (be[blk(i, nu)], 0, 0)),
        ],
        out_specs=pl.BlockSpec((EXPERT_ROWS, PACKED), lambda i, be, nu: (i, 0)),
        scratch_shapes=[pltpu.VMEM((D_FF, D_MODEL), BF16)],
    )
    return pl.pallas_call(
        _down_body,
        grid_spec=grid_spec,
        out_shape=jax.ShapeDtypeStruct((act.shape[0], PACKED), U32),
        compiler_params=_params(1, VMEM_LIMIT),
        name="expert_down",
    )(block_e, n_used, act, w_down, b_down)


def kernel(x_prompt, x_sample, cache_k, cache_v, norm1_g, w_in, attn_sinks, rel_bias, gm_ln_g, gm_ln_b, gm_ws,
           gm_bs, w_branch_attn, w_branch_gm, w_out, norm2_g, router_w, router_b, w_gate_up, b_gate_up, w_down,
           b_down, final_g):
    depth = w_in.shape[0]
    assert depth == 1, "single-layer stack"
    batch, seq, _ = x_prompt.shape
    dec_batch, dec_seq, _ = x_sample.shape
    n_cache = cache_k.shape[2]
    assert n_cache == WINDOW and seq % WINDOW == 0 and CHUNK % dec_seq == 0
    t_p, t_s = batch * seq, dec_batch * dec_seq
    l = 0

    w_proj = _split_w_in(w_in[l])
    wba, wbg, wout = w_branch_attn[l].astype(BF16), w_branch_gm[l].astype(BF16), w_out[l].astype(BF16)
    rw = router_w[l].astype(BF16)
    g1, g2, fg = norm1_g[l][None], norm2_g[l][None], final_g[None]
    rb = router_b[l][None]
    ln_g, ln_b = gm_ln_g[l][None], gm_ln_b[l][None]
    reps = CHUNK // dec_seq
    ws_p, bs_p = gm_ws[l], gm_bs[l][:, :, None]
    ws_s = jnp.tile(gm_ws[l][:, :dec_seq, :dec_seq], (1, reps, reps))
    bs_s = jnp.tile(gm_bs[l][:, :dec_seq], (1, reps))[:, :, None]
    b_gu = b_gate_up[l][:, None, :]
    b_dn = b_down[l][:, None, :]
    sinks = attn_sinks[l]

    xp = x_prompt.reshape(t_p, D_MODEL)
    xs_ = x_sample.reshape(t_s, D_MODEL)

    raw_p, u_p, gates_p = _in_project(xp, g1, *w_proj)
    ao_p = _attention_prompt(raw_p, rel_bias, sinks, seq)
    (gm_p,) = _spatial_gating(u_p, raw_p, ln_g, ln_b, ws_p, bs_p, CHUNK, False)

    raw_s, u_s, gates_s = _in_project(xs_, g1, *w_proj)
    ao_s, k_s, v_s = _attention_sample(raw_s.reshape(dec_batch, dec_seq, RAW_WIDTH),
                                       cache_k[l].reshape(dec_batch, n_cache, KV_WIDTH),
                                       cache_v[l].reshape(dec_batch, n_cache, KV_WIDTH), rel_bias, sinks)
    gm_s, vn_s = _spatial_gating(u_s, raw_s, ln_g, ln_b, ws_s, bs_s, dec_seq, True)

    zero_counts = jnp.zeros((1, N_EXPERTS), I32)
    h_p, hnp_p, e_p, gate_p, rank_p, counts_p = _block_tail(xp, ao_p, gm_p, gates_p, wba, wbg, wout, g2, rw, rb,
                                                            zero_counts)
    h_s, hnp_s, e_s, gate_s, rank_s, counts = _block_tail(xs_, ao_s.reshape(t_s, ATTN_WIDTH), gm_s, gates_s, wba, wbg,
                                                          wout, g2, rw, rb, counts_p)

    counts = counts[0]
    padded = (counts + EXPERT_ROWS - 1) // EXPERT_ROWS * EXPERT_ROWS
    pad_end = jnp.cumsum(padded)
    pad_start = pad_end - padded
    n_blocks = (t_p + t_s) * TOP_K // EXPERT_ROWS + N_EXPERTS
    block_row = jnp.arange(n_blocks, dtype=I32) * EXPERT_ROWS
    block_e = jnp.minimum(jnp.sum(pad_end[None, :] <= block_row[:, None], axis=1), N_EXPERTS - 1).astype(I32)
    n_used = (pad_end[-1:] // EXPERT_ROWS).astype(I32)
    dest_p = (pad_start[e_p[:, :TOP_K]] + rank_p[:, :TOP_K]).astype(I32).reshape(-1)
    dest_s = (pad_start[e_s[:, :TOP_K]] + rank_s[:, :TOP_K]).astype(I32).reshape(-1)

    xs_sorted = jnp.zeros((n_blocks * EXPERT_ROWS, PACKED), U32)
    xs_sorted = _dispatch(dest_p, hnp_p, xs_sorted)
    xs_sorted = _dispatch(dest_s, hnp_s, xs_sorted)
    act = _gate_up(block_e, n_used, xs_sorted, w_gate_up[l], b_gu)
    yb = _down(block_e, n_used, act, w_down[l], b_dn)
    y_p = _combine(dest_p, gate_p, h_p, fg, yb)
    y_s = _combine(dest_s, gate_s, h_s, fg, yb)

    keep = min(WINDOW, seq)
    kv_last = raw_p.reshape(batch, seq, RAW_WIDTH)[:, -keep:, RAW_K_COL:]
    k_p = kv_last[:, :, :KV_WIDTH].reshape(batch, keep, N_KV_HEADS, HEAD_DIM)
    v_p = kv_last[:, :, KV_WIDTH:].reshape(batch, keep, N_KV_HEADS, HEAD_DIM)
    k_s = k_s.reshape(dec_batch, n_cache, N_KV_HEADS, HEAD_DIM)
    v_s = v_s.reshape(dec_batch, n_cache, N_KV_HEADS, HEAD_DIM)
    return (y_p.reshape(batch, seq, D_MODEL), y_s.reshape(dec_batch, dec_seq, D_MODEL),
            k_p[None], v_p[None], k_s[None], v_s[None], vn_s.reshape(1, dec_batch, dec_seq, GM_WIDTH))
```

```python
import functools
import math

import numpy as np
import jax
import jax.numpy as jnp
from jax import lax
from jax.experimental import pallas as pl
from jax.experimental.pallas import tpu as pltpu

F32 = jnp.float32
BF16 = jnp.bfloat16
U32 = jnp.uint32
I32 = jnp.int32

D_MODEL = 2048
HEAD_DIM = 64
N_HEADS = 16
N_KV_HEADS = 4
GQA_GROUP = N_HEADS // N_KV_HEADS
ATTN_WIDTH = N_HEADS * HEAD_DIM
KV_WIDTH = N_KV_HEADS * HEAD_DIM
WINDOW = 128
NUM_BUCKETS = 32
MAX_DISTANCE = 128
GM_WIDTH = D_MODEL // 2
GM_GROUPS = 8
CHUNK = 128
N_EXPERTS = 32
TOP_K = 4
D_FF = D_MODEL
SWIGLU_ALPHA = 1.702
SWIGLU_LIMIT = 7.0
NORM_EPS = 1e-5
NEG_INF = -1e30
IN_WIDTH = ATTN_WIDTH + 2 * KV_WIDTH + 2 * GM_WIDTH + 2 * D_MODEL

LANES = 128
IN_TM = 1024
IN_TN = 512
ACT_TN = 1024
GM_ROWS = 512
TAIL_TM = 256
EXPERT_ROWS = 256
FF_CHUNK = 1024
N_FF_CHUNKS = D_FF // FF_CHUNK
SAMPLE_SEQS = 8
SAMPLE_KEYS = 256
PACKED = D_MODEL // 2
ROW_TILE = PACKED // LANES
assert ROW_TILE == 8
VMEM_LIMIT = 56 * 1024 * 1024

RAW_WIDTH = ATTN_WIDTH + GM_WIDTH + 2 * KV_WIDTH
RAW_Q_BLOCK = 0
RAW_GV_BLOCK = ATTN_WIDTH // GM_WIDTH
RAW_KV_BLOCK = (ATTN_WIDTH + GM_WIDTH) // (2 * KV_WIDTH)
RAW_K_COL = ATTN_WIDTH + GM_WIDTH


def _params(n_axes, vmem=None):
    return pltpu.CompilerParams(dimension_semantics=("arbitrary",) * n_axes, vmem_limit_bytes=vmem)


def _pack_pair(lo, hi):
    lo_b = lax.bitcast_convert_type(lo.astype(BF16).astype(F32), U32)
    hi_b = lax.bitcast_convert_type(hi.astype(BF16).astype(F32), U32)
    return (lo_b >> 16) | (hi_b & jnp.uint32(0xFFFF0000))


def _unpack_pair(w):
    lo = lax.bitcast_convert_type(w << 16, F32)
    hi = lax.bitcast_convert_type(w & jnp.uint32(0xFFFF0000), F32)
    return lo, hi


def _store_row_tiles(ref, row0, lo, hi):
    n = lo.shape[0]
    words = _pack_pair(lo, hi)
    for s in range(ROW_TILE):
        ref[pl.ds(row0 + s, n, stride=ROW_TILE), :] = words[:, s * LANES:(s + 1) * LANES]


def _load_row_tiles(ref, row0, n):
    parts = [_unpack_pair(ref[pl.ds(row0 + s, n, stride=ROW_TILE), :]) for s in range(ROW_TILE)]
    return (jnp.concatenate([p[0] for p in parts], axis=1), jnp.concatenate([p[1] for p in parts], axis=1))


def _rms(x, g):
    return (x * lax.rsqrt(jnp.mean(x * x, axis=-1, keepdims=True) + NORM_EPS)) * g


def _proj_norm_body(x_ref, g_ref, w_ref, o_ref, xn_ref):
    @pl.when(pl.program_id(1) == 0)
    def _():
        xn_ref[...] = _rms(x_ref[...], g_ref[...]).astype(BF16)

    o_ref[...] = jnp.dot(xn_ref[...], w_ref[...], preferred_element_type=F32)


def _proj_body(xn_ref, w_ref, o_ref, *, epilogue):
    o_ref[...] = epilogue(jnp.dot(xn_ref[...], w_ref[...], preferred_element_type=F32)).astype(o_ref.dtype)


def _split_w_in(w_in):
    q_end = ATTN_WIDTH
    kv_end = q_end + 2 * KV_WIDTH
    gu_end = kv_end + GM_WIDTH
    gv_end = gu_end + GM_WIDTH
    w_raw = jnp.concatenate([w_in[:, :q_end], w_in[:, gu_end:gv_end], w_in[:, q_end:kv_end]], axis=1)
    return w_raw.astype(BF16), w_in[:, kv_end:gu_end].astype(BF16), w_in[:, gv_end:].astype(BF16)


def _in_project(x2d, g1, w_raw, w_gelu, w_sig):
    t = x2d.shape[0]

    raw, xn = pl.pallas_call(
        _proj_norm_body,
        grid=(t // IN_TM, RAW_WIDTH // IN_TN),
        in_specs=[pl.BlockSpec((IN_TM, D_MODEL), lambda i, j: (i, 0)),
                  pl.BlockSpec((1, D_MODEL), lambda i, j: (0, 0)),
                  pl.BlockSpec((D_MODEL, IN_TN), lambda i, j: (0, j))],
        out_specs=[pl.BlockSpec((IN_TM, IN_TN), lambda i, j: (i, j)),
                   pl.BlockSpec((IN_TM, D_MODEL), lambda i, j: (i, 0))],
        out_shape=[jax.ShapeDtypeStruct((t, RAW_WIDTH), F32), jax.ShapeDtypeStruct((t, D_MODEL), BF16)],
        compiler_params=_params(2, VMEM_LIMIT),
        name="in_project_raw",
    )(x2d, g1, w_raw)

    def act_call(w, epilogue, name):
        width = w.shape[1]
        return pl.pallas_call(
            functools.partial(_proj_body, epilogue=epilogue),
            grid=(t // IN_TM, width // ACT_TN),
            in_specs=[pl.BlockSpec((IN_TM, D_MODEL), lambda i, j: (i, 0)),
                      pl.BlockSpec((D_MODEL, ACT_TN), lambda i, j: (0, j))],
            out_specs=pl.BlockSpec((IN_TM, ACT_TN), lambda i, j: (i, j)),
            out_shape=jax.ShapeDtypeStruct((t, width), BF16),
            compiler_params=_params(2, VMEM_LIMIT),
            name=name,
        )(xn, w)

    return raw, act_call(w_gelu, jax.nn.gelu, "in_project_gelu"), act_call(w_sig, jax.nn.sigmoid, "in_project_sigmoid")


def _bucket_table(n_q, n_k, offset, n_valid_k):
    i = np.arange(n_q)[:, None]
    j = np.arange(n_k)[None, :]
    dist = offset + i - j
    n = np.maximum(dist, 0)
    max_exact = NUM_BUCKETS // 2
    nf = np.maximum(n, 1).astype(np.float64)
    large = max_exact + (np.log(nf / max_exact) / math.log(MAX_DISTANCE / max_exact)
                         * (NUM_BUCKETS - max_exact)).astype(np.int32)
    large = np.minimum(large, NUM_BUCKETS - 1)
    bucket = np.where(n < max_exact, n, large)
    valid = (dist >= 0) & (dist < WINDOW) & (j < n_valid_k)
    return jnp.asarray(np.where(valid, bucket, -1).astype(np.int32))


def _fill_bias(bkt_ref, rb_ref, bias_ref):
    bkt = bkt_ref[...]

    def per_head(h, carry):
        def per_bucket(b, acc):
            return jnp.where(bkt == b, rb_ref[b, h], acc)
        bias_ref[h] = lax.fori_loop(0, NUM_BUCKETS, per_bucket, jnp.full(bkt.shape, NEG_INF, F32))
        return carry

    lax.fori_loop(0, N_HEADS, per_head, 0)


def _sink_softmax(s, sink):
    m = jnp.maximum(jnp.max(s, axis=-1, keepdims=True), sink)
    p = jnp.exp(s - m)
    return p / (jnp.sum(p, axis=-1, keepdims=True) + jnp.exp(sink - m))


def _attn_prompt_body(q_ref, kvc_ref, kvp_ref, bkt_ref, rb_ref, sink_ref, o_ref, bias_ref, *, blocks_per_seq):
    s = pl.program_id(0)

    @pl.when(s == 0)
    def _():
        _fill_bias(bkt_ref, rb_ref, bias_ref)

    first = (s % blocks_per_seq) == 0
    kc = kvc_ref[...]
    kp = kvp_ref[...]
    k2 = jnp.concatenate([kp[:, :KV_WIDTH], kc[:, :KV_WIDTH]], axis=0).astype(BF16)
    v2 = jnp.concatenate([kp[:, KV_WIDTH:], kc[:, KV_WIDTH:]], axis=0).astype(BF16)
    col = lax.broadcasted_iota(I32, (WINDOW, 2 * WINDOW), 1)
    no_prev = (col < WINDOW) & first
    q = (q_ref[...] * (HEAD_DIM ** -0.5)).astype(BF16)
    for kh in range(N_KV_HEADS):
        k_h = k2[:, kh * HEAD_DIM:(kh + 1) * HEAD_DIM]
        v_h = v2[:, kh * HEAD_DIM:(kh + 1) * HEAD_DIM]
        heads = [kh * GQA_GROUP + g for g in range(GQA_GROUP)]
        qg = jnp.concatenate([q[:, h * HEAD_DIM:(h + 1) * HEAD_DIM] for h in heads], axis=0)
        sc = lax.dot_general(qg, k_h, (((1,), (1,)), ((), ())), preferred_element_type=F32)
        probs = []
        for g, h in enumerate(heads):
            bias = jnp.where(no_prev, NEG_INF, bias_ref[h])
            probs.append(_sink_softmax(sc[g * WINDOW:(g + 1) * WINDOW] + bias, sink_ref[h]).astype(BF16))
        o = jnp.dot(jnp.concatenate(probs, axis=0), v_h, preferred_element_type=F32)
        for g, h in enumerate(heads):
            o_ref[:, h * HEAD_DIM:(h + 1) * HEAD_DIM] = o[g * WINDOW:(g + 1) * WINDOW].astype(o_ref.dtype)


def _attention_prompt(raw, rel_bias, sinks, seq):
    t = raw.shape[0]
    n_blocks = t // WINDOW
    bkt = _bucket_table(WINDOW, 2 * WINDOW, WINDOW, 2 * WINDOW)
    smem = pl.BlockSpec(memory_space=pltpu.SMEM)
    return pl.pallas_call(
        functools.partial(_attn_prompt_body, blocks_per_seq=seq // WINDOW),
        grid=(n_blocks,),
        in_specs=[
            pl.BlockSpec((WINDOW, ATTN_WIDTH), lambda s: (s, RAW_Q_BLOCK)),
            pl.BlockSpec((WINDOW, 2 * KV_WIDTH), lambda s: (s, RAW_KV_BLOCK)),
            pl.BlockSpec((WINDOW, 2 * KV_WIDTH), lambda s: (jnp.maximum(s - 1, 0), RAW_KV_BLOCK)),
            pl.BlockSpec((WINDOW, 2 * WINDOW), lambda s: (0, 0)),
            smem, smem,
        ],
        out_specs=pl.BlockSpec((WINDOW, ATTN_WIDTH), lambda s: (s, 0)),
        out_shape=jax.ShapeDtypeStruct((t, ATTN_WIDTH), BF16),
        scratch_shapes=[pltpu.VMEM((N_HEADS, WINDOW, 2 * WINDOW), F32)],
        compiler_params=_params(1),
        name="attention_prompt",
    )(raw, raw, raw, bkt, rel_bias, sinks)


def _attn_sample_body(q_ref, kvn_ref, ck_ref, cv_ref, bkt_ref, rb_ref, sink_ref, o_ref, nk_ref, nv_ref, bias_ref,
                      *, n_q, n_cache):
    @pl.when(pl.program_id(0) == 0)
    def _():
        _fill_bias(bkt_ref, rb_ref, bias_ref)

    kvn = kvn_ref[...]
    zeros = jnp.zeros((SAMPLE_SEQS, SAMPLE_KEYS - n_cache - n_q, KV_WIDTH), F32)
    k_all = jnp.concatenate([ck_ref[...], kvn[:, :, :KV_WIDTH], zeros], axis=1)
    v_all = jnp.concatenate([cv_ref[...], kvn[:, :, KV_WIDTH:], zeros], axis=1)
    nk_ref[...] = k_all[:, n_q:n_q + n_cache]
    nv_ref[...] = v_all[:, n_q:n_q + n_cache]
    k = k_all.astype(BF16)
    v = v_all.astype(BF16)
    q = q_ref[...] * (HEAD_DIM ** -0.5)
    for kh in range(N_KV_HEADS):
        k_h = k[:, :, kh * HEAD_DIM:(kh + 1) * HEAD_DIM]
        v_h = v[:, :, kh * HEAD_DIM:(kh + 1) * HEAD_DIM]
        heads = [kh * GQA_GROUP + g for g in range(GQA_GROUP)]
        qg = jnp.concatenate([q[:, :, h * HEAD_DIM:(h + 1) * HEAD_DIM] for h in heads], axis=1).astype(BF16)
        sc = jnp.einsum('bqd,bsd->bqs', qg, k_h, preferred_element_type=F32)
        bias = jnp.concatenate([bias_ref[h] for h in heads], axis=0)
        sink = jnp.concatenate([jnp.full((n_q, 1), sink_ref[h], F32) for h in heads], axis=0)
        p = _sink_softmax(sc + bias[None], sink[None]).astype(BF16)
        o = jnp.einsum('bqs,bsd->bqd', p, v_h, preferred_element_type=F32)
        for g, h in enumerate(heads):
            o_ref[:, :, h * HEAD_DIM:(h + 1) * HEAD_DIM] = o[:, g * n_q:(g + 1) * n_q, :]


def _attention_sample(raw3, cache_k, cache_v, rel_bias, sinks):
    n_seq, n_q, _ = raw3.shape
    n_cache = cache_k.shape[1]
    bkt = _bucket_table(n_q, SAMPLE_KEYS, n_cache, n_cache + n_q)
    smem = pl.BlockSpec(memory_space=pltpu.SMEM)
    cache_spec = pl.BlockSpec((SAMPLE_SEQS, n_cache, KV_WIDTH), lambda s: (s, 0, 0))
    o_spec = pl.BlockSpec((SAMPLE_SEQS, n_q, ATTN_WIDTH), lambda s: (s, 0, 0))
    return pl.pallas_call(
        functools.partial(_attn_sample_body, n_q=n_q, n_cache=n_cache),
        grid=(n_seq // SAMPLE_SEQS,),
        in_specs=[pl.BlockSpec((SAMPLE_SEQS, n_q, ATTN_WIDTH), lambda s: (s, 0, RAW_Q_BLOCK)),
                  pl.BlockSpec((SAMPLE_SEQS, n_q, 2 * KV_WIDTH), lambda s: (s, 0, RAW_KV_BLOCK)),
                  cache_spec, cache_spec,
                  pl.BlockSpec((n_q, SAMPLE_KEYS), lambda s: (0, 0)), smem, smem],
        out_specs=[o_spec, cache_spec, cache_spec],
        out_shape=[jax.ShapeDtypeStruct((n_seq, n_q, ATTN_WIDTH), F32),
                   jax.ShapeDtypeStruct(cache_k.shape, F32), jax.ShapeDtypeStruct(cache_v.shape, F32)],
        scratch_shapes=[pltpu.VMEM((N_HEADS, n_q, SAMPLE_KEYS), F32)],
        compiler_params=_params(1),
        name="attention_sample",
    )(raw3, raw3, cache_k, cache_v, bkt, rel_bias, sinks)


def _gating_body(u_ref, gv_ref, lng_ref, lnb_ref, w_ref, bs_ref, *rest, seq_rows, emit_vn):
    if emit_vn:
        o_ref, vn_ref, wm_ref = rest
    else:
        o_ref, wm_ref = rest

    @pl.when(pl.program_id(0) == 0)
    def _():
        r = lax.broadcasted_iota(I32, (CHUNK, CHUNK), 0)
        c = lax.broadcasted_iota(I32, (CHUNK, CHUNK), 1)
        keep = c <= r
        if seq_rows < CHUNK:
            shift = seq_rows.bit_length() - 1
            assert seq_rows == 1 << shift
            keep = keep & ((r >> shift) == (c >> shift))
        for g in range(GM_GROUPS):
            wm_ref[g] = jnp.where(keep, w_ref[g], 0.0).astype(BF16)

    a = jax.nn.gelu(gv_ref[...])
    mu = jnp.mean(a, axis=-1, keepdims=True)
    var = jnp.mean(jnp.square(a - mu), axis=-1, keepdims=True)
    vn = ((a - mu) * lax.rsqrt(var + NORM_EPS)) * lng_ref[...] + lnb_ref[...]
    if emit_vn:
        vn_ref[...] = vn
    vnb = vn.astype(BF16)
    for ch in range(GM_ROWS // CHUNK):
        rows = slice(ch * CHUNK, (ch + 1) * CHUNK)
        for g in range(GM_GROUPS):
            cols = slice(g * CHUNK, (g + 1) * CHUNK)
            mixed = jnp.dot(wm_ref[g], vnb[rows, cols], preferred_element_type=F32) + bs_ref[g]
            o_ref[rows, cols] = (u_ref[rows, cols].astype(F32) * mixed).astype(BF16)


def _spatial_gating(u, raw, ln_g, ln_b, ws, bs_col, seq_rows, emit_vn):
    t = u.shape[0]
    row_spec = pl.BlockSpec((GM_ROWS, GM_WIDTH), lambda i: (i, 0))
    vec_spec = pl.BlockSpec((1, GM_WIDTH), lambda i: (0, 0))
    out_shape = [jax.ShapeDtypeStruct((t, GM_WIDTH), BF16)]
    out_specs = [row_spec]
    if emit_vn:
        out_shape.append(jax.ShapeDtypeStruct((t, GM_WIDTH), F32))
        out_specs.append(row_spec)
    return pl.pallas_call(
        functools.partial(_gating_body, seq_rows=seq_rows, emit_vn=emit_vn),
        grid=(t // GM_ROWS,),
        in_specs=[row_spec, pl.BlockSpec((GM_ROWS, GM_WIDTH), lambda i: (i, RAW_GV_BLOCK)), vec_spec, vec_spec,
                  pl.BlockSpec((GM_GROUPS, CHUNK, CHUNK), lambda i: (0, 0, 0)),
                  pl.BlockSpec((GM_GROUPS, CHUNK, 1), lambda i: (0, 0, 0))],
        out_specs=out_specs,
        out_shape=out_shape,
        scratch_shapes=[pltpu.VMEM((GM_GROUPS, CHUNK, CHUNK), BF16)],
        compiler_params=_params(1),
        name="spatial_gating",
    )(u, raw, ln_g, ln_b, ws, bs_col)


def _lanes4(vals, dtype):
    lane = lax.broadcasted_iota(I32, (vals[0].shape[0], LANES), 1)
    out = jnp.zeros((vals[0].shape[0], LANES), dtype)
    for k, v in enumerate(vals):
        out = jnp.where(lane == k, v.astype(dtype), out)
    return out


def _tail_body(x_ref, ao_ref, gm_ref, sga_ref, sgg_ref, wba_ref, wbg_ref, wout_ref, g2_ref, rw_ref, rb_ref,
               cin_ref, h_ref, hnp_ref, e_ref, gate_ref, rank_ref, cout_ref, carry_ref):
    @pl.when(pl.program_id(0) == 0)
    def _():
        carry_ref[...] = cin_ref[...].astype(F32)

    a = jnp.dot(ao_ref[...].astype(BF16), wba_ref[...], preferred_element_type=F32)
    m = jnp.dot(gm_ref[...], wbg_ref[...], preferred_element_type=F32)
    merged = sga_ref[...].astype(F32) * a + sgg_ref[...].astype(F32) * m
    h = x_ref[...] + jnp.dot(merged.astype(BF16), wout_ref[...], preferred_element_type=F32)
    h_ref[...] = h
    hn = _rms(h, g2_ref[...])
    _store_row_tiles(hnp_ref, 0, hn[:, :PACKED], hn[:, PACKED:])

    logits = jnp.dot(hn.astype(BF16), rw_ref[...], preferred_element_type=F32) + rb_ref[...]
    tm = logits.shape[0]
    lane = lax.broadcasted_iota(I32, (tm, N_EXPERTS), 1).astype(F32)
    work = logits
    tops, idxs, hots = [], [], []
    for _ in range(TOP_K):
        mx = jnp.max(work, axis=-1, keepdims=True)
        idx = jnp.min(jnp.where(work == mx, lane, float(N_EXPERTS)), axis=-1, keepdims=True)
        hot = lane == idx
        tops.append(mx)
        idxs.append(idx)
        hots.append(hot)
        work = jnp.where(hot, -jnp.inf, work)
    ex = [jnp.exp(v - tops[0]) for v in tops]
    den = ex[0] + ex[1] + ex[2] + ex[3]
    gates = [v / den for v in ex]

    onehot = sum(jnp.where(hot, 1.0, 0.0) for hot in hots)
    r = lax.broadcasted_iota(I32, (tm, tm), 0)
    c = lax.broadcasted_iota(I32, (tm, tm), 1)
    earlier = jnp.where(c < r, 1.0, 0.0).astype(BF16)
    prefix = jnp.dot(earlier, onehot.astype(BF16), preferred_element_type=F32) + carry_ref[...]
    ranks = [jnp.sum(jnp.where(hot, prefix, 0.0), axis=-1, keepdims=True) for hot in hots]
    carry_ref[...] = carry_ref[...] + jnp.sum(onehot, axis=0, keepdims=True)

    e_ref[...] = _lanes4(idxs, I32)
    gate_ref[...] = _lanes4(gates, F32)
    rank_ref[...] = _lanes4(ranks, I32)
    cout_ref[...] = carry_ref[...].astype(I32)


def _block_tail(x2d, attn_o, gm_o, gates, wba, wbg, wout, g2, rw, rb, counts_in):
    t = x2d.shape[0]
    tm = TAIL_TM
    row = lambda w, col=0: pl.BlockSpec((tm, w), lambda i: (i, col))
    once = lambda shape: pl.BlockSpec(shape, lambda i: (0, 0), pipeline_mode=pl.Buffered(1))
    small = lambda shape: pl.BlockSpec(shape, lambda i: (0, 0))
    return pl.pallas_call(
        _tail_body,
        grid=(t // tm,),
        in_specs=[row(D_MODEL), row(ATTN_WIDTH), row(GM_WIDTH), row(D_MODEL, 0), row(D_MODEL, 1),
                  once((ATTN_WIDTH, D_MODEL)), once((GM_WIDTH, D_MODEL)), once((D_MODEL, D_MODEL)),
                  small((1, D_MODEL)), small((D_MODEL, N_EXPERTS)), small((1, N_EXPERTS)), small((1, N_EXPERTS))],
        out_specs=[row(D_MODEL), pl.BlockSpec((tm * ROW_TILE, LANES), lambda i: (i, 0)),
                   row(LANES), row(LANES), row(LANES), small((1, N_EXPERTS))],
        out_shape=[
            jax.ShapeDtypeStruct((t, D_MODEL), F32),
            jax.ShapeDtypeStruct((t * ROW_TILE, LANES), U32),
            jax.ShapeDtypeStruct((t, LANES), I32),
            jax.ShapeDtypeStruct((t, LANES), F32),
            jax.ShapeDtypeStruct((t, LANES), I32),
            jax.ShapeDtypeStruct((1, N_EXPERTS), I32),
        ],
        scratch_shapes=[pltpu.VMEM((1, N_EXPERTS), F32)],
        compiler_params=_params(1, VMEM_LIMIT),
        name="block_tail",
    )(x2d, attn_o, gm_o, gates, gates, wba, wbg, wout, g2, rw, rb, counts_in)


_WAIT_UNROLL = 16


def _wait_rows(make_copy, n):
    def body(_, carry):
        for _ in range(_WAIT_UNROLL):
            make_copy().wait()
        return carry
    lax.fori_loop(0, n // _WAIT_UNROLL, body, 0)


def _token_rows(ref, t):
    return ref.at[pl.ds(pl.multiple_of(t * ROW_TILE, ROW_TILE), ROW_TILE), :]


def _dispatch_body(dest_ref, zb_ref, nz_ref, hp_ref, hs_ref, xs_ref, zero_ref, sem, zsem, *, prompt_tiles):
    i = pl.program_id(0)
    tm = hp_ref.shape[0] // ROW_TILE
    block_rows = EXPERT_ROWS * ROW_TILE

    @pl.when(i == 0)
    def _():
        zero_ref[...] = jnp.zeros_like(zero_ref)

        def zero_copy(j):
            row = pl.multiple_of(zb_ref[j] * block_rows, block_rows)
            return pltpu.make_async_copy(zero_ref, xs_ref.at[pl.ds(row, block_rows), :], zsem)

        def start(j, carry):
            zero_copy(j).start()
            return carry

        def wait(j, carry):
            zero_copy(j).wait()
            return carry

        lax.fori_loop(0, nz_ref[0], start, 0)
        lax.fori_loop(0, nz_ref[0], wait, 0)

    def scatter(hn_ref):
        def issue(t, carry):
            for k in range(TOP_K):
                pltpu.make_async_copy(_token_rows(hn_ref, t), _token_rows(xs_ref, dest_ref[t * TOP_K + k]), sem).start()
            return carry

        lax.fori_loop(0, tm, issue, 0, unroll=4)
        _wait_rows(lambda: pltpu.make_async_copy(_token_rows(hn_ref, 0), _token_rows(xs_ref, 0), sem), tm * TOP_K)

    @pl.when(i < prompt_tiles)
    def _():
        scatter(hp_ref)

    @pl.when(i >= prompt_tiles)
    def _():
        scatter(hs_ref)


def _dispatch(dest_flat, zero_blocks, n_zero, hnp_p, hnp_s, n_slots):
    tm = TAIL_TM
    tiles_p = hnp_p.shape[0] // (tm * ROW_TILE)
    tiles_s = hnp_s.shape[0] // (tm * ROW_TILE)
    smem = pl.BlockSpec(memory_space=pltpu.SMEM)
    return pl.pallas_call(
        functools.partial(_dispatch_body, prompt_tiles=tiles_p),
        grid=(tiles_p + tiles_s,),
        in_specs=[pl.BlockSpec((tm * TOP_K,), lambda i: (i,), memory_space=pltpu.SMEM), smem, smem,
                  pl.BlockSpec((tm * ROW_TILE, LANES), lambda i: (jnp.minimum(i, tiles_p - 1), 0)),
                  pl.BlockSpec((tm * ROW_TILE, LANES), lambda i: (jnp.maximum(i - tiles_p, 0), 0))],
        out_specs=pl.BlockSpec(memory_space=pl.ANY),
        out_shape=jax.ShapeDtypeStruct((n_slots * ROW_TILE, LANES), U32),
        scratch_shapes=[pltpu.VMEM((EXPERT_ROWS * ROW_TILE, LANES), U32),
                        pltpu.SemaphoreType.DMA, pltpu.SemaphoreType.DMA],
        compiler_params=_params(1),
        name="dispatch",
    )(dest_flat, zero_blocks, n_zero, hnp_p, hnp_s)


def _combine_body(dest_ref, gate_ref, h_ref, fg_ref, yb_ref, o_ref, buf_ref, sem):
    tm = h_ref.shape[0]

    def issue(t, carry):
        for k in range(TOP_K):
            pltpu.make_async_copy(_token_rows(yb_ref, dest_ref[t * TOP_K + k]), _token_rows(buf_ref, k * tm + t),
                                  sem).start()
        return carry

    lax.fori_loop(0, tm, issue, 0, unroll=4)
    _wait_rows(lambda: pltpu.make_async_copy(_token_rows(yb_ref, 0), _token_rows(buf_ref, 0), sem), tm * TOP_K)

    gate = gate_ref[...]
    f_lo = jnp.zeros((tm, PACKED), F32)
    f_hi = jnp.zeros((tm, PACKED), F32)
    for k in range(TOP_K):
        lo, hi = _load_row_tiles(buf_ref, k * tm * ROW_TILE, tm)
        gk = gate[:, k:k + 1]
        f_lo = f_lo + gk * lo
        f_hi = f_hi + gk * hi
    y = h_ref[...] + jnp.concatenate([f_lo, f_hi], axis=1)
    o_ref[...] = _rms(y, fg_ref[...])


def _combine(dest_flat, gate, h, final_g, yb):
    t = h.shape[0]
    tm = TAIL_TM
    return pl.pallas_call(
        _combine_body,
        grid=(t // tm,),
        in_specs=[pl.BlockSpec((tm * TOP_K,), lambda i: (i,), memory_space=pltpu.SMEM),
                  pl.BlockSpec((tm, LANES), lambda i: (i, 0)),
                  pl.BlockSpec((tm, D_MODEL), lambda i: (i, 0)),
                  pl.BlockSpec((1, D_MODEL), lambda i: (0, 0)),
                  pl.BlockSpec(memory_space=pl.ANY)],
        out_specs=pl.BlockSpec((tm, D_MODEL), lambda i: (i, 0)),
        out_shape=jax.ShapeDtypeStruct((t, D_MODEL), F32),
        scratch_shapes=[pltpu.VMEM((TOP_K * tm * ROW_TILE, LANES), U32), pltpu.SemaphoreType.DMA],
        compiler_params=_params(1),
        name="combine",
    )(dest_flat, gate, h, final_g, yb)


def _expert_changed(i, be_ref):
    return (i == 0) | (be_ref[i] != be_ref[jnp.maximum(i - 1, 0)])


def _gate_up_body(be_ref, nu_ref, run_ref, nxt_ref, last_ref, nr_ref, xs_ref, bg_ref, bl_ref, w_hbm, o_ref,
                  wbuf_ref, wgb_ref, wlb_ref, sem):
    c = pl.program_id(0)
    i = pl.program_id(1)
    live = i < nu_ref[0]

    def w_copy(e, chunk, part, slot):
        col = pl.multiple_of((part * N_FF_CHUNKS + chunk) * FF_CHUNK, FF_CHUNK)
        return pltpu.make_async_copy(w_hbm.at[e, :, pl.ds(col, FF_CHUNK)], wbuf_ref.at[slot, part], sem.at[slot, part])

    @pl.when(live & _expert_changed(i, be_ref))
    def _():
        slot = (c * nr_ref[0] + run_ref[i]) & 1

        @pl.when((c == 0) & (i == 0))
        def _():
            for part in range(2):
                w_copy(be_ref[0], 0, part, 0).start()

        for part in range(2):
            w_copy(be_ref[i], c, part, slot).wait()

        @pl.when(jnp.logical_not((last_ref[i] == 1) & (c == N_FF_CHUNKS - 1)))
        def _():
            for part in range(2):
                w_copy(nxt_ref[i], c + last_ref[i], part, 1 - slot).start()

        wgb_ref[...] = wbuf_ref[slot, 0].astype(BF16)
        wlb_ref[...] = wbuf_ref[slot, 1].astype(BF16)

    @pl.when(live)
    def _():
        lo, hi = _load_row_tiles(xs_ref, 0, EXPERT_ROWS)
        x = jnp.concatenate([lo, hi], axis=1).astype(BF16)
        glu = jnp.dot(x, wgb_ref[...], preferred_element_type=F32) + bg_ref[...]
        lin = jnp.dot(x, wlb_ref[...], preferred_element_type=F32) + bl_ref[...]
        glu = jnp.minimum(glu, SWIGLU_LIMIT)
        lin = jnp.clip(lin, -SWIGLU_LIMIT, SWIGLU_LIMIT)
        o_ref[...] = (glu * jax.nn.sigmoid(SWIGLU_ALPHA * glu) * (lin + 1.0)).astype(BF16)

    @pl.when(jnp.logical_not(live))
    def _():
        o_ref[...] = jnp.zeros_like(o_ref)


def _run_tables(block_e, n_used):
    n_blocks = block_e.shape[0]
    idx = jnp.arange(n_blocks, dtype=I32)
    live = idx < n_used[0]
    first = live & ((idx == 0) | (block_e != jnp.roll(block_e, 1)))
    run = jnp.cumsum(first.astype(I32)) - 1
    first_pos = jnp.where(first, idx, n_blocks)
    later = jnp.concatenate([lax.cummin(first_pos, reverse=True)[1:], jnp.full((1,), n_blocks, I32)])
    is_last = later >= n_blocks
    nxt = jnp.where(is_last, block_e[0], block_e[jnp.minimum(later, n_blocks - 1)])
    return run.astype(I32), nxt.astype(I32), is_last.astype(I32), jnp.sum(first.astype(I32)).reshape(1)


def _gate_up(tables, xs, w_gate_up, b_gate_up):
    n_blocks = xs.shape[0] // (EXPERT_ROWS * ROW_TILE)
    blk = lambda i, nu: jnp.minimum(i, nu[0] - 1)
    bias = lambda part: pl.BlockSpec((None, 1, FF_CHUNK),
                                     lambda c, i, be, nu, *_: (be[blk(i, nu)], 0, part * N_FF_CHUNKS + c))
    grid_spec = pltpu.PrefetchScalarGridSpec(
        num_scalar_prefetch=len(tables),
        grid=(N_FF_CHUNKS, n_blocks),
        in_specs=[
            pl.BlockSpec((EXPERT_ROWS * ROW_TILE, LANES), lambda c, i, be, nu, *_: (blk(i, nu), 0)),
            bias(0), bias(1),
            pl.BlockSpec(memory_space=pl.ANY),
        ],
        out_specs=pl.BlockSpec((EXPERT_ROWS, FF_CHUNK), lambda c, i, *_: (i, c)),
        scratch_shapes=[pltpu.VMEM((2, 2, D_MODEL, FF_CHUNK), F32),
                        pltpu.VMEM((D_MODEL, FF_CHUNK), BF16), pltpu.VMEM((D_MODEL, FF_CHUNK), BF16),
                        pltpu.SemaphoreType.DMA((2, 2))],
    )
    return pl.pallas_call(
        _gate_up_body,
        grid_spec=grid_spec,
        out_shape=jax.ShapeDtypeStruct((n_blocks * EXPERT_ROWS, D_FF), BF16),
        compiler_params=_params(2, VMEM_LIMIT),
        name="expert_gate_up",
    )(*tables, xs, b_gate_up, b_gate_up, w_gate_up)


def _down_body(be_ref, nu_ref, run_ref, nxt_ref, last_ref, nr_ref, a_ref, b_ref, w_hbm, o_ref, wbuf_ref, wb_ref, sem):
    del nr_ref
    i = pl.program_id(0)
    live = i < nu_ref[0]

    def w_copy(e, slot):
        return pltpu.make_async_copy(w_hbm.at[e], wbuf_ref.at[slot], sem.at[slot])

    @pl.when(live & _expert_changed(i, be_ref))
    def _():
        slot = run_ref[i] & 1

        @pl.when(i == 0)
        def _():
            w_copy(be_ref[0], 0).start()

        w_copy(be_ref[i], slot).wait()

        @pl.when(last_ref[i] == 0)
        def _():
            w_copy(nxt_ref[i], 1 - slot).start()

        wb_ref[...] = wbuf_ref[slot].astype(BF16)

    @pl.when(live)
    def _():
        y = jnp.dot(a_ref[...], wb_ref[...], preferred_element_type=F32) + b_ref[...]
        _store_row_tiles(o_ref, 0, y[:, :PACKED], y[:, PACKED:])

    @pl.when(jnp.logical_not(live))
    def _():
        o_ref[...] = jnp.zeros_like(o_ref)


def _down(tables, act, w_down, b_down):
    n_blocks = act.shape[0] // EXPERT_ROWS
    blk = lambda i, nu: jnp.minimum(i, nu[0] - 1)
    grid_spec = pltpu.PrefetchScalarGridSpec(
        num_scalar_prefetch=len(tables),
        grid=(n_blocks,),
        in_specs=[
            pl.BlockSpec((EXPERT_ROWS, D_FF), lambda i, be, nu, *_: (blk(i, nu), 0)),
            pl.BlockSpec((None, 1, D_MODEL), lambda i, be, nu, *_: (be[blk(i, nu)], 0, 0)),
            pl.BlockSpec(memory_space=pl.ANY),
        ],
        out_specs=pl.BlockSpec((EXPERT_ROWS * ROW_TILE, LANES), lambda i, *_: (i, 0)),
        scratch_shapes=[pltpu.VMEM((2, D_FF, D_MODEL), F32), pltpu.VMEM((D_FF, D_MODEL), BF16),
                        pltpu.SemaphoreType.DMA((2,))],
    )
    return pl.pallas_call(
        _down_body,
        grid_spec=grid_spec,
        out_shape=jax.ShapeDtypeStruct((n_blocks * EXPERT_ROWS * ROW_TILE, LANES), U32),
        compiler_params=_params(1, VMEM_LIMIT),
        name="expert_down",
    )(*tables, act, b_down, w_down)


def kernel(x_prompt, x_sample, cache_k, cache_v, norm1_g, w_in, attn_sinks, rel_bias, gm_ln_g, gm_ln_b, gm_ws,
           gm_bs, w_branch_attn, w_branch_gm, w_out, norm2_g, router_w, router_b, w_gate_up, b_gate_up, w_down,
           b_down, final_g):
    depth = w_in.shape[0]
    assert depth == 1, "single-layer stack"
    batch, seq, _ = x_prompt.shape
    dec_batch, dec_seq, _ = x_sample.shape
    n_cache = cache_k.shape[2]
    assert n_cache == WINDOW and seq % WINDOW == 0 and CHUNK % dec_seq == 0
    t_p, t_s = batch * seq, dec_batch * dec_seq
    l = 0

    w_proj = _split_w_in(w_in[l])
    wba, wbg, wout = w_branch_attn[l].astype(BF16), w_branch_gm[l].astype(BF16), w_out[l].astype(BF16)
    rw = router_w[l].astype(BF16)
    g1, g2, fg = norm1_g[l][None], norm2_g[l][None], final_g[None]
    rb = router_b[l][None]
    ln_g, ln_b = gm_ln_g[l][None], gm_ln_b[l][None]
    reps = CHUNK // dec_seq
    ws_p, bs_p = gm_ws[l], gm_bs[l][:, :, None]
    ws_s = jnp.tile(gm_ws[l][:, :dec_seq, :dec_seq], (1, reps, reps))
    bs_s = jnp.tile(gm_bs[l][:, :dec_seq], (1, reps))[:, :, None]
    b_gu = b_gate_up[l][:, None, :]
    b_dn = b_down[l][:, None, :]
    sinks = attn_sinks[l]

    xp = x_prompt.reshape(t_p, D_MODEL)
    xs_ = x_sample.reshape(t_s, D_MODEL)

    raw_p, u_p, gates_p = _in_project(xp, g1, *w_proj)
    ao_p = _attention_prompt(raw_p, rel_bias, sinks, seq)
    (gm_p,) = _spatial_gating(u_p, raw_p, ln_g, ln_b, ws_p, bs_p, CHUNK, False)

    raw_s, u_s, gates_s = _in_project(xs_, g1, *w_proj)
    ao_s, k_s, v_s = _attention_sample(raw_s.reshape(dec_batch, dec_seq, RAW_WIDTH),
                                       cache_k[l].reshape(dec_batch, n_cache, KV_WIDTH),
                                       cache_v[l].reshape(dec_batch, n_cache, KV_WIDTH), rel_bias, sinks)
    gm_s, vn_s = _spatial_gating(u_s, raw_s, ln_g, ln_b, ws_s, bs_s, dec_seq, True)

    zero_counts = jnp.zeros((1, N_EXPERTS), I32)
    h_p, hnp_p, e_p, gate_p, rank_p, counts_p = _block_tail(xp, ao_p, gm_p, gates_p, wba, wbg, wout, g2, rw, rb,
                                                            zero_counts)
    h_s, hnp_s, e_s, gate_s, rank_s, counts = _block_tail(xs_, ao_s.reshape(t_s, ATTN_WIDTH), gm_s, gates_s, wba, wbg,
                                                          wout, g2, rw, rb, counts_p)

    counts = counts[0]
    padded = (counts + EXPERT_ROWS - 1) // EXPERT_ROWS * EXPERT_ROWS
    pad_end = jnp.cumsum(padded)
    pad_start = pad_end - padded
    n_blocks = (t_p + t_s) * TOP_K // EXPERT_ROWS + N_EXPERTS
    block_row = jnp.arange(n_blocks, dtype=I32) * EXPERT_ROWS
    block_e = jnp.minimum(jnp.sum(pad_end[None, :] <= block_row[:, None], axis=1), N_EXPERTS - 1).astype(I32)
    n_used = (pad_end[-1:] // EXPERT_ROWS).astype(I32)
    dest_p = (pad_start[e_p[:, :TOP_K]] + rank_p[:, :TOP_K]).astype(I32).reshape(-1)
    dest_s = (pad_start[e_s[:, :TOP_K]] + rank_s[:, :TOP_K]).astype(I32).reshape(-1)

    last_block = pad_end // EXPERT_ROWS - 1
    tail_block = n_used[0] + jnp.arange(N_EXPERTS, dtype=I32)
    cand = jnp.concatenate([last_block.astype(I32), tail_block])
    cand_ok = jnp.concatenate([padded > 0, tail_block < n_blocks])
    zero_blocks = cand[jnp.argsort(jnp.logical_not(cand_ok))]
    n_zero = jnp.sum(cand_ok.astype(I32)).reshape(1)
    tables = (block_e, n_used) + _run_tables(block_e, n_used)

    xs_sorted = _dispatch(jnp.concatenate([dest_p, dest_s]), zero_blocks, n_zero, hnp_p, hnp_s,
                          n_blocks * EXPERT_ROWS)
    act = _gate_up(tables, xs_sorted, w_gate_up[l], b_gu)
    yb = _down(tables, act, w_down[l], b_dn)
    y_p = _combine(dest_p, gate_p, h_p, fg, yb)
    y_s = _combine(dest_s, gate_s, h_s, fg, yb)

    keep = min(WINDOW, seq)
    kv_last = raw_p.reshape(batch, seq, RAW_WIDTH)[:, -keep:, RAW_K_COL:]
    k_p = kv_last[:, :, :KV_WIDTH].reshape(batch, keep, N_KV_HEADS, HEAD_DIM)
    v_p = kv_last[:, :, KV_WIDTH:].reshape(batch, keep, N_KV_HEADS, HEAD_DIM)
    k_s = k_s.reshape(dec_batch, n_cache, N_KV_HEADS, HEAD_DIM)
    v_s = v_s.reshape(dec_batch, n_cache, N_KV_HEADS, HEAD_DIM)
    return (y_p.reshape(batch, seq, D_MODEL), y_s.reshape(dec_batch, dec_seq, D_MODEL),
            k_p[None], v_p[None], k_s[None], v_s[None], vn_s.reshape(1, dec_batch, dec_seq, GM_WIDTH))
```

```python
import functools
import math

import numpy as np
import jax
import jax.numpy as jnp
from jax import lax
from jax.experimental import pallas as pl
from jax.experimental.pallas import tpu as pltpu

F32 = jnp.float32
BF16 = jnp.bfloat16
U32 = jnp.uint32
I32 = jnp.int32

D_MODEL = 2048
HEAD_DIM = 64
N_HEADS = 16
N_KV_HEADS = 4
GQA_GROUP = N_HEADS // N_KV_HEADS
ATTN_WIDTH = N_HEADS * HEAD_DIM
KV_WIDTH = N_KV_HEADS * HEAD_DIM
WINDOW = 128
NUM_BUCKETS = 32
MAX_DISTANCE = 128
GM_WIDTH = D_MODEL // 2
GM_GROUPS = 8
CHUNK = 128
N_EXPERTS = 32
TOP_K = 4
D_FF = D_MODEL
SWIGLU_ALPHA = 1.702
SWIGLU_LIMIT = 7.0
NORM_EPS = 1e-5
NEG_INF = -1e30
IN_WIDTH = ATTN_WIDTH + 2 * KV_WIDTH + 2 * GM_WIDTH + 2 * D_MODEL

LANES = 128
IN_TM = 1024
IN_TN = 512
ACT_TN = 1024
GM_ROWS = 512
TAIL_TM = 256
EXPERT_ROWS = 256
FF_CHUNK = 1024
N_FF_CHUNKS = D_FF // FF_CHUNK
STEP_BLOCKS = 2
SAMPLE_SEQS = 8
SAMPLE_KEYS = 256
PACKED = D_MODEL // 2
RANK_BITS = 20
Q_BLOCKS = 2
ROW_TILE = PACKED // LANES
assert ROW_TILE == 8
VMEM_LIMIT = 56 * 1024 * 1024

RAW_WIDTH = ATTN_WIDTH + GM_WIDTH + 2 * KV_WIDTH
RAW_Q_BLOCK = 0
RAW_GV_BLOCK = ATTN_WIDTH // GM_WIDTH
RAW_KV_BLOCK = (ATTN_WIDTH + GM_WIDTH) // (2 * KV_WIDTH)
RAW_K_COL = ATTN_WIDTH + GM_WIDTH


def _params(n_axes, vmem=None):
    return pltpu.CompilerParams(dimension_semantics=("arbitrary",) * n_axes, vmem_limit_bytes=vmem)


def _pack_pair(lo, hi):
    lo_b = lax.bitcast_convert_type(lo.astype(BF16).astype(F32), U32)
    hi_b = lax.bitcast_convert_type(hi.astype(BF16).astype(F32), U32)
    return (lo_b >> 16) | (hi_b & jnp.uint32(0xFFFF0000))


def _unpack_pair(w):
    lo = lax.bitcast_convert_type(w << 16, F32)
    hi = lax.bitcast_convert_type(w & jnp.uint32(0xFFFF0000), F32)
    return lo, hi


def _store_row_tiles(ref, row0, lo, hi):
    n = lo.shape[0]
    words = _pack_pair(lo, hi)
    for s in range(ROW_TILE):
        ref[pl.ds(row0 + s, n, stride=ROW_TILE), :] = words[:, s * LANES:(s + 1) * LANES]


def _load_row_tiles(ref, row0, n):
    parts = [_unpack_pair(ref[pl.ds(row0 + s, n, stride=ROW_TILE), :]) for s in range(ROW_TILE)]
    return (jnp.concatenate([p[0] for p in parts], axis=1), jnp.concatenate([p[1] for p in parts], axis=1))


def _rms(x, g):
    return (x * lax.rsqrt(jnp.mean(x * x, axis=-1, keepdims=True) + NORM_EPS)) * g


def _proj_norm_body(x_ref, g_ref, w_ref, o_ref, xn_ref):
    @pl.when(pl.program_id(1) == 0)
    def _():
        xn_ref[...] = _rms(x_ref[...], g_ref[...]).astype(BF16)

    o_ref[...] = jnp.dot(xn_ref[...], w_ref[...], preferred_element_type=F32)


def _proj_body(xn_ref, w_ref, o_ref, *, epilogue):
    o_ref[...] = epilogue(jnp.dot(xn_ref[...], w_ref[...], preferred_element_type=F32)).astype(o_ref.dtype)


def _split_w_in(w_in):
    q_end = ATTN_WIDTH
    kv_end = q_end + 2 * KV_WIDTH
    gu_end = kv_end + GM_WIDTH
    gv_end = gu_end + GM_WIDTH
    w_raw = jnp.concatenate([w_in[:, :q_end], w_in[:, gu_end:gv_end], w_in[:, q_end:kv_end]], axis=1)
    return w_raw.astype(BF16), w_in[:, kv_end:gu_end].astype(BF16), w_in[:, gv_end:].astype(BF16)


def _in_project(x2d, g1, w_raw, w_gelu, w_sig):
    t = x2d.shape[0]

    raw, xn = pl.pallas_call(
        _proj_norm_body,
        grid=(t // IN_TM, RAW_WIDTH // IN_TN),
        in_specs=[pl.BlockSpec((IN_TM, D_MODEL), lambda i, j: (i, 0)),
                  pl.BlockSpec((1, D_MODEL), lambda i, j: (0, 0)),
                  pl.BlockSpec((D_MODEL, IN_TN), lambda i, j: (0, j))],
        out_specs=[pl.BlockSpec((IN_TM, IN_TN), lambda i, j: (i, j)),
                   pl.BlockSpec((IN_TM, D_MODEL), lambda i, j: (i, 0))],
        out_shape=[jax.ShapeDtypeStruct((t, RAW_WIDTH), F32), jax.ShapeDtypeStruct((t, D_MODEL), BF16)],
        compiler_params=_params(2, VMEM_LIMIT),
        name="in_project_raw",
    )(x2d, g1, w_raw)

    def act_call(w, epilogue, name):
        width = w.shape[1]
        return pl.pallas_call(
            functools.partial(_proj_body, epilogue=epilogue),
            grid=(t // IN_TM, width // ACT_TN),
            in_specs=[pl.BlockSpec((IN_TM, D_MODEL), lambda i, j: (i, 0)),
                      pl.BlockSpec((D_MODEL, ACT_TN), lambda i, j: (0, j))],
            out_specs=pl.BlockSpec((IN_TM, ACT_TN), lambda i, j: (i, j)),
            out_shape=jax.ShapeDtypeStruct((t, width), BF16),
            compiler_params=_params(2, VMEM_LIMIT),
            name=name,
        )(xn, w)

    return raw, act_call(w_gelu, jax.nn.gelu, "in_project_gelu"), act_call(w_sig, jax.nn.sigmoid, "in_project_sigmoid")


def _bucket_table(n_q, n_k, offset, n_valid_k):
    i = np.arange(n_q)[:, None]
    j = np.arange(n_k)[None, :]
    dist = offset + i - j
    n = np.maximum(dist, 0)
    max_exact = NUM_BUCKETS // 2
    nf = np.maximum(n, 1).astype(np.float64)
    large = max_exact + (np.log(nf / max_exact) / math.log(MAX_DISTANCE / max_exact)
                         * (NUM_BUCKETS - max_exact)).astype(np.int32)
    large = np.minimum(large, NUM_BUCKETS - 1)
    bucket = np.where(n < max_exact, n, large)
    valid = (dist >= 0) & (dist < WINDOW) & (j < n_valid_k)
    return jnp.asarray(np.where(valid, bucket, -1).astype(np.int32))


def _fill_bias(bkt_ref, rb_ref, bias_ref):
    bkt = bkt_ref[...]

    def per_head(h, carry):
        def per_bucket(b, acc):
            return jnp.where(bkt == b, rb_ref[b, h], acc)
        bias_ref[h] = lax.fori_loop(0, NUM_BUCKETS, per_bucket, jnp.full(bkt.shape, NEG_INF, F32))
        return carry

    lax.fori_loop(0, N_HEADS, per_head, 0)


def _sink_softmax(s, sink):
    m = jnp.maximum(jnp.max(s, axis=-1, keepdims=True), sink)
    p = jnp.exp(s - m)
    return p / (jnp.sum(p, axis=-1, keepdims=True) + jnp.exp(sink - m))


def _attn_prompt_body(q_ref, kvc_ref, kvp_ref, bkt_ref, rb_ref, sink_ref, o_ref, bias_ref, *, blocks_per_seq):
    s = pl.program_id(0)

    @pl.when(s == 0)
    def _():
        _fill_bias(bkt_ref, rb_ref, bias_ref)

    kv = jnp.concatenate([kvp_ref[...], kvc_ref[...]], axis=0)
    k3 = kv[:, :KV_WIDTH].astype(BF16)
    v3 = kv[:, KV_WIDTH:].astype(BF16)
    col = lax.broadcasted_iota(I32, (WINDOW, 2 * WINDOW), 1)
    for b in range(Q_BLOCKS):
        rows = slice(b * WINDOW, (b + 1) * WINDOW)
        first = ((s * Q_BLOCKS + b) % blocks_per_seq) == 0
        no_prev = (col < WINDOW) & first
        k2 = k3[b * WINDOW:(b + 2) * WINDOW]
        v2 = v3[b * WINDOW:(b + 2) * WINDOW]
        q = (q_ref[rows, :] * (HEAD_DIM ** -0.5)).astype(BF16)
        for kh in range(N_KV_HEADS):
            k_h = k2[:, kh * HEAD_DIM:(kh + 1) * HEAD_DIM]
            v_h = v2[:, kh * HEAD_DIM:(kh + 1) * HEAD_DIM]
            heads = [kh * GQA_GROUP + g for g in range(GQA_GROUP)]
            qg = jnp.concatenate([q[:, h * HEAD_DIM:(h + 1) * HEAD_DIM] for h in heads], axis=0)
            sc = lax.dot_general(qg, k_h, (((1,), (1,)), ((), ())), preferred_element_type=F32)
            probs = []
            for g, h in enumerate(heads):
                bias = jnp.where(no_prev, NEG_INF, bias_ref[h])
                probs.append(_sink_softmax(sc[g * WINDOW:(g + 1) * WINDOW] + bias, sink_ref[h]).astype(BF16))
            o = jnp.dot(jnp.concatenate(probs, axis=0), v_h, preferred_element_type=F32)
            for g, h in enumerate(heads):
                o_ref[rows, h * HEAD_DIM:(h + 1) * HEAD_DIM] = o[g * WINDOW:(g + 1) * WINDOW].astype(o_ref.dtype)


def _attention_prompt(raw, rel_bias, sinks, seq):
    t = raw.shape[0]
    step_rows = Q_BLOCKS * WINDOW
    assert t % step_rows == 0
    bkt = _bucket_table(WINDOW, 2 * WINDOW, WINDOW, 2 * WINDOW)
    smem = pl.BlockSpec(memory_space=pltpu.SMEM)
    return pl.pallas_call(
        functools.partial(_attn_prompt_body, blocks_per_seq=seq // WINDOW),
        grid=(t // step_rows,),
        in_specs=[
            pl.BlockSpec((step_rows, ATTN_WIDTH), lambda s: (s, RAW_Q_BLOCK)),
            pl.BlockSpec((step_rows, 2 * KV_WIDTH), lambda s: (s, RAW_KV_BLOCK)),
            pl.BlockSpec((WINDOW, 2 * KV_WIDTH), lambda s: (jnp.maximum(s * Q_BLOCKS - 1, 0), RAW_KV_BLOCK)),
            pl.BlockSpec((WINDOW, 2 * WINDOW), lambda s: (0, 0)),
            smem, smem,
        ],
        out_specs=pl.BlockSpec((step_rows, ATTN_WIDTH), lambda s: (s, 0)),
        out_shape=jax.ShapeDtypeStruct((t, ATTN_WIDTH), BF16),
        scratch_shapes=[pltpu.VMEM((N_HEADS, WINDOW, 2 * WINDOW), F32)],
        compiler_params=_params(1),
        name="attention_prompt",
    )(raw, raw, raw, bkt, rel_bias, sinks)


def _attn_sample_body(q_ref, kvn_ref, ck_ref, cv_ref, bkt_ref, rb_ref, sink_ref, o_ref, nk_ref, nv_ref, bias_ref,
                      *, n_q, n_cache):
    @pl.when(pl.program_id(0) == 0)
    def _():
        _fill_bias(bkt_ref, rb_ref, bias_ref)

    kvn = kvn_ref[...]
    zeros = jnp.zeros((SAMPLE_SEQS, SAMPLE_KEYS - n_cache - n_q, KV_WIDTH), F32)
    k_all = jnp.concatenate([ck_ref[...], kvn[:, :, :KV_WIDTH], zeros], axis=1)
    v_all = jnp.concatenate([cv_ref[...], kvn[:, :, KV_WIDTH:], zeros], axis=1)
    nk_ref[...] = k_all[:, n_q:n_q + n_cache]
    nv_ref[...] = v_all[:, n_q:n_q + n_cache]
    k = k_all.astype(BF16)
    v = v_all.astype(BF16)
    q = q_ref[...] * (HEAD_DIM ** -0.5)
    for kh in range(N_KV_HEADS):
        k_h = k[:, :, kh * HEAD_DIM:(kh + 1) * HEAD_DIM]
        v_h = v[:, :, kh * HEAD_DIM:(kh + 1) * HEAD_DIM]
        heads = [kh * GQA_GROUP + g for g in range(GQA_GROUP)]
        qg = jnp.concatenate([q[:, :, h * HEAD_DIM:(h + 1) * HEAD_DIM] for h in heads], axis=1).astype(BF16)
        sc = jnp.einsum('bqd,bsd->bqs', qg, k_h, preferred_element_type=F32)
        bias = jnp.concatenate([bias_ref[h] for h in heads], axis=0)
        sink = jnp.concatenate([jnp.full((n_q, 1), sink_ref[h], F32) for h in heads], axis=0)
        p = _sink_softmax(sc + bias[None], sink[None]).astype(BF16)
        o = jnp.einsum('bqs,bsd->bqd', p, v_h, preferred_element_type=F32)
        for g, h in enumerate(heads):
            o_ref[:, :, h * HEAD_DIM:(h + 1) * HEAD_DIM] = o[:, g * n_q:(g + 1) * n_q, :]


def _attention_sample(raw3, cache_k, cache_v, rel_bias, sinks):
    n_seq, n_q, _ = raw3.shape
    n_cache = cache_k.shape[1]
    bkt = _bucket_table(n_q, SAMPLE_KEYS, n_cache, n_cache + n_q)
    smem = pl.BlockSpec(memory_space=pltpu.SMEM)
    cache_spec = pl.BlockSpec((SAMPLE_SEQS, n_cache, KV_WIDTH), lambda s: (s, 0, 0))
    o_spec = pl.BlockSpec((SAMPLE_SEQS, n_q, ATTN_WIDTH), lambda s: (s, 0, 0))
    return pl.pallas_call(
        functools.partial(_attn_sample_body, n_q=n_q, n_cache=n_cache),
        grid=(n_seq // SAMPLE_SEQS,),
        in_specs=[pl.BlockSpec((SAMPLE_SEQS, n_q, ATTN_WIDTH), lambda s: (s, 0, RAW_Q_BLOCK)),
                  pl.BlockSpec((SAMPLE_SEQS, n_q, 2 * KV_WIDTH), lambda s: (s, 0, RAW_KV_BLOCK)),
                  cache_spec, cache_spec,
                  pl.BlockSpec((n_q, SAMPLE_KEYS), lambda s: (0, 0)), smem, smem],
        out_specs=[o_spec, cache_spec, cache_spec],
        out_shape=[jax.ShapeDtypeStruct((n_seq, n_q, ATTN_WIDTH), F32),
                   jax.ShapeDtypeStruct(cache_k.shape, F32), jax.ShapeDtypeStruct(cache_v.shape, F32)],
        scratch_shapes=[pltpu.VMEM((N_HEADS, n_q, SAMPLE_KEYS), F32)],
        compiler_params=_params(1),
        name="attention_sample",
    )(raw3, raw3, cache_k, cache_v, bkt, rel_bias, sinks)


def _gating_body(u_ref, gv_ref, lng_ref, lnb_ref, w_ref, bs_ref, *rest, seq_rows, emit_vn):
    if emit_vn:
        o_ref, vn_ref, wm_ref = rest
    else:
        o_ref, wm_ref = rest

    @pl.when(pl.program_id(0) == 0)
    def _():
        r = lax.broadcasted_iota(I32, (CHUNK, CHUNK), 0)
        c = lax.broadcasted_iota(I32, (CHUNK, CHUNK), 1)
        keep = c <= r
        if seq_rows < CHUNK:
            shift = seq_rows.bit_length() - 1
            assert seq_rows == 1 << shift
            keep = keep & ((r >> shift) == (c >> shift))
        for g in range(GM_GROUPS):
            wm_ref[g] = jnp.where(keep, w_ref[g], 0.0).astype(BF16)

    a = jax.nn.gelu(gv_ref[...])
    mu = jnp.mean(a, axis=-1, keepdims=True)
    var = jnp.mean(jnp.square(a - mu), axis=-1, keepdims=True)
    vn = ((a - mu) * lax.rsqrt(var + NORM_EPS)) * lng_ref[...] + lnb_ref[...]
    if emit_vn:
        vn_ref[...] = vn
    vnb = vn.astype(BF16)
    for ch in range(GM_ROWS // CHUNK):
        rows = slice(ch * CHUNK, (ch + 1) * CHUNK)
        for g in range(GM_GROUPS):
            cols = slice(g * CHUNK, (g + 1) * CHUNK)
            mixed = jnp.dot(wm_ref[g], vnb[rows, cols], preferred_element_type=F32) + bs_ref[g]
            o_ref[rows, cols] = (u_ref[rows, cols].astype(F32) * mixed).astype(BF16)


def _spatial_gating(u, raw, ln_g, ln_b, ws, bs_col, seq_rows, emit_vn):
    t = u.shape[0]
    row_spec = pl.BlockSpec((GM_ROWS, GM_WIDTH), lambda i: (i, 0))
    vec_spec = pl.BlockSpec((1, GM_WIDTH), lambda i: (0, 0))
    out_shape = [jax.ShapeDtypeStruct((t, GM_WIDTH), BF16)]
    out_specs = [row_spec]
    if emit_vn:
        out_shape.append(jax.ShapeDtypeStruct((t, GM_WIDTH), F32))
        out_specs.append(row_spec)
    return pl.pallas_call(
        functools.partial(_gating_body, seq_rows=seq_rows, emit_vn=emit_vn),
        grid=(t // GM_ROWS,),
        in_specs=[row_spec, pl.BlockSpec((GM_ROWS, GM_WIDTH), lambda i: (i, RAW_GV_BLOCK)), vec_spec, vec_spec,
                  pl.BlockSpec((GM_GROUPS, CHUNK, CHUNK), lambda i: (0, 0, 0)),
                  pl.BlockSpec((GM_GROUPS, CHUNK, 1), lambda i: (0, 0, 0))],
        out_specs=out_specs,
        out_shape=out_shape,
        scratch_shapes=[pltpu.VMEM((GM_GROUPS, CHUNK, CHUNK), BF16)],
        compiler_params=_params(1),
        name="spatial_gating",
    )(u, raw, ln_g, ln_b, ws, bs_col)


def _lanes4(vals, dtype):
    lane = lax.broadcasted_iota(I32, (vals[0].shape[0], LANES), 1)
    out = jnp.zeros((vals[0].shape[0], LANES), dtype)
    for k, v in enumerate(vals):
        out = jnp.where(lane == k, v.astype(dtype), out)
    return out


def _tail_body(x_ref, ao_ref, gm_ref, sga_ref, sgg_ref, wba_ref, wbg_ref, wout_ref, g2_ref, rw_ref, rb_ref,
               cin_ref, h_ref, hnp_ref, code_ref, gate_ref, cout_ref, carry_ref):
    @pl.when(pl.program_id(0) == 0)
    def _():
        carry_ref[...] = cin_ref[...].astype(F32)

    a = jnp.dot(ao_ref[...].astype(BF16), wba_ref[...], preferred_element_type=F32)
    m = jnp.dot(gm_ref[...], wbg_ref[...], preferred_element_type=F32)
    merged = sga_ref[...].astype(F32) * a + sgg_ref[...].astype(F32) * m
    h = x_ref[...] + jnp.dot(merged.astype(BF16), wout_ref[...], preferred_element_type=F32)
    h_ref[...] = h
    hn = _rms(h, g2_ref[...])
    _store_row_tiles(hnp_ref, 0, hn[:, :PACKED], hn[:, PACKED:])

    logits = jnp.dot(hn.astype(BF16), rw_ref[...], preferred_element_type=F32) + rb_ref[...]
    tm = logits.shape[0]
    lane = lax.broadcasted_iota(I32, (tm, N_EXPERTS), 1).astype(F32)
    work = logits
    tops, idxs, hots = [], [], []
    for _ in range(TOP_K):
        mx = jnp.max(work, axis=-1, keepdims=True)
        idx = jnp.min(jnp.where(work == mx, lane, float(N_EXPERTS)), axis=-1, keepdims=True)
        hot = lane == idx
        tops.append(mx)
        idxs.append(idx)
        hots.append(hot)
        work = jnp.where(hot, -jnp.inf, work)
    ex = [jnp.exp(v - tops[0]) for v in tops]
    den = ex[0] + ex[1] + ex[2] + ex[3]
    gates = [v / den for v in ex]

    onehot = sum(jnp.where(hot, 1.0, 0.0) for hot in hots)
    r = lax.broadcasted_iota(I32, (tm, tm), 0)
    c = lax.broadcasted_iota(I32, (tm, tm), 1)
    earlier = jnp.where(c < r, 1.0, 0.0).astype(BF16)
    prefix = jnp.dot(earlier, onehot.astype(BF16), preferred_element_type=F32) + carry_ref[...]
    ranks = [jnp.sum(jnp.where(hot, prefix, 0.0), axis=-1, keepdims=True) for hot in hots]
    carry_ref[...] = carry_ref[...] + jnp.sum(onehot, axis=0, keepdims=True)

    code_ref[...] = _lanes4([e.astype(I32) * (1 << RANK_BITS) + r.astype(I32) for e, r in zip(idxs, ranks)], I32)
    gate_ref[...] = _lanes4(gates, F32)
    cout_ref[...] = carry_ref[...].astype(I32)


def _block_tail(x2d, attn_o, gm_o, gates, wba, wbg, wout, g2, rw, rb, counts_in):
    t = x2d.shape[0]
    tm = TAIL_TM
    row = lambda w, col=0: pl.BlockSpec((tm, w), lambda i: (i, col))
    once = lambda shape: pl.BlockSpec(shape, lambda i: (0, 0), pipeline_mode=pl.Buffered(1))
    small = lambda shape: pl.BlockSpec(shape, lambda i: (0, 0))
    return pl.pallas_call(
        _tail_body,
        grid=(t // tm,),
        in_specs=[row(D_MODEL), row(ATTN_WIDTH), row(GM_WIDTH), row(D_MODEL, 0), row(D_MODEL, 1),
                  once((ATTN_WIDTH, D_MODEL)), once((GM_WIDTH, D_MODEL)), once((D_MODEL, D_MODEL)),
                  small((1, D_MODEL)), small((D_MODEL, N_EXPERTS)), small((1, N_EXPERTS)), small((1, N_EXPERTS))],
        out_specs=[row(D_MODEL), pl.BlockSpec((tm * ROW_TILE, LANES), lambda i: (i, 0)),
                   row(LANES), row(LANES), small((1, N_EXPERTS))],
        out_shape=[
            jax.ShapeDtypeStruct((t, D_MODEL), F32),
            jax.ShapeDtypeStruct((t * ROW_TILE, LANES), U32),
            jax.ShapeDtypeStruct((t, LANES), I32),
            jax.ShapeDtypeStruct((t, LANES), F32),
            jax.ShapeDtypeStruct((1, N_EXPERTS), I32),
        ],
        scratch_shapes=[pltpu.VMEM((1, N_EXPERTS), F32)],
        compiler_params=_params(1, VMEM_LIMIT),
        name="block_tail",
    )(x2d, attn_o, gm_o, gates, gates, wba, wbg, wout, g2, rw, rb, counts_in)


_WAIT_UNROLL = 16


def _wait_rows(make_copy, n):
    def body(_, carry):
        for _ in range(_WAIT_UNROLL):
            make_copy().wait()
        return carry
    lax.fori_loop(0, n // _WAIT_UNROLL, body, 0)


def _token_rows(ref, t):
    return ref.at[pl.ds(pl.multiple_of(t * ROW_TILE, ROW_TILE), ROW_TILE), :]


def _slot(code, start_ref):
    return start_ref[lax.shift_right_logical(code, RANK_BITS)] + (code & ((1 << RANK_BITS) - 1))


def _dispatch_body(code_ref, start_ref, zb_ref, nz_ref, hp_ref, hs_ref, xs_ref, zero_ref, sem, zsem, *, prompt_tiles):
    i = pl.program_id(0)
    tm = hp_ref.shape[0] // ROW_TILE
    block_rows = EXPERT_ROWS * ROW_TILE

    @pl.when(i == 0)
    def _():
        zero_ref[...] = jnp.zeros_like(zero_ref)

        def zero_copy(j):
            row = pl.multiple_of(zb_ref[j] * block_rows, block_rows)
            return pltpu.make_async_copy(zero_ref, xs_ref.at[pl.ds(row, block_rows), :], zsem)

        def start(j, carry):
            zero_copy(j).start()
            return carry

        def wait(j, carry):
            zero_copy(j).wait()
            return carry

        lax.fori_loop(0, nz_ref[0], start, 0)
        lax.fori_loop(0, nz_ref[0], wait, 0)

    def scatter(hn_ref):
        def issue(t, carry):
            for k in range(TOP_K):
                d = _slot(code_ref[t * TOP_K + k], start_ref)
                pltpu.make_async_copy(_token_rows(hn_ref, t), _token_rows(xs_ref, d), sem).start()
            return carry

        lax.fori_loop(0, tm, issue, 0, unroll=4)
        _wait_rows(lambda: pltpu.make_async_copy(_token_rows(hn_ref, 0), _token_rows(xs_ref, 0), sem), tm * TOP_K)

    @pl.when(i < prompt_tiles)
    def _():
        scatter(hp_ref)

    @pl.when(i >= prompt_tiles)
    def _():
        scatter(hs_ref)


def _dispatch(codes, starts, zero_blocks, n_zero, hnp_p, hnp_s, n_slots):
    tm = TAIL_TM
    tiles_p = hnp_p.shape[0] // (tm * ROW_TILE)
    tiles_s = hnp_s.shape[0] // (tm * ROW_TILE)
    smem = pl.BlockSpec(memory_space=pltpu.SMEM)
    return pl.pallas_call(
        functools.partial(_dispatch_body, prompt_tiles=tiles_p),
        grid=(tiles_p + tiles_s,),
        in_specs=[pl.BlockSpec((tm * TOP_K,), lambda i: (i,), memory_space=pltpu.SMEM), smem, smem, smem,
                  pl.BlockSpec((tm * ROW_TILE, LANES), lambda i: (jnp.minimum(i, tiles_p - 1), 0)),
                  pl.BlockSpec((tm * ROW_TILE, LANES), lambda i: (jnp.maximum(i - tiles_p, 0), 0))],
        out_specs=pl.BlockSpec(memory_space=pl.ANY),
        out_shape=jax.ShapeDtypeStruct((n_slots * ROW_TILE, LANES), U32),
        scratch_shapes=[pltpu.VMEM((EXPERT_ROWS * ROW_TILE, LANES), U32),
                        pltpu.SemaphoreType.DMA, pltpu.SemaphoreType.DMA],
        compiler_params=_params(1),
        name="dispatch",
    )(codes, starts, zero_blocks, n_zero, hnp_p, hnp_s)


def _combine_body(code_ref, next_code_ref, start_ref, gate_ref, h_ref, fg_ref, yb_ref, o_ref, buf_ref, sem):
    s = pl.program_id(0)
    tm = h_ref.shape[0]
    half = s & 1

    def gather(codes, into):
        base = into * (TOP_K * tm)

        def issue(t, carry):
            for k in range(TOP_K):
                d = _slot(codes[t * TOP_K + k], start_ref)
                pltpu.make_async_copy(_token_rows(yb_ref, d), _token_rows(buf_ref, base + k * tm + t),
                                      sem.at[into]).start()
            return carry

        lax.fori_loop(0, tm, issue, 0, unroll=4)

    @pl.when(s == 0)
    def _():
        gather(code_ref, 0)

    @pl.when(s + 1 < pl.num_programs(0))
    def _():
        gather(next_code_ref, 1 - half)

    _wait_rows(lambda: pltpu.make_async_copy(_token_rows(yb_ref, 0), _token_rows(buf_ref, 0), sem.at[half]),
               tm * TOP_K)

    gate = gate_ref[...]
    f_lo = jnp.zeros((tm, PACKED), F32)
    f_hi = jnp.zeros((tm, PACKED), F32)
    for k in range(TOP_K):
        lo, hi = _load_row_tiles(buf_ref, (half * TOP_K + k) * tm * ROW_TILE, tm)
        gk = gate[:, k:k + 1]
        f_lo = f_lo + gk * lo
        f_hi = f_hi + gk * hi
    y = h_ref[...] + jnp.concatenate([f_lo, f_hi], axis=1)
    o_ref[...] = _rms(y, fg_ref[...])


def _combine(codes, starts, gate, h, final_g, yb):
    t = h.shape[0]
    tm = TAIL_TM
    n_tiles = t // tm
    return pl.pallas_call(
        _combine_body,
        grid=(n_tiles,),
        in_specs=[pl.BlockSpec((tm * TOP_K,), lambda i: (i,), memory_space=pltpu.SMEM),
                  pl.BlockSpec((tm * TOP_K,), lambda i: (jnp.minimum(i + 1, n_tiles - 1),), memory_space=pltpu.SMEM),
                  pl.BlockSpec(memory_space=pltpu.SMEM),
                  pl.BlockSpec((tm, LANES), lambda i: (i, 0)),
                  pl.BlockSpec((tm, D_MODEL), lambda i: (i, 0)),
                  pl.BlockSpec((1, D_MODEL), lambda i: (0, 0)),
                  pl.BlockSpec(memory_space=pl.ANY)],
        out_specs=pl.BlockSpec((tm, D_MODEL), lambda i: (i, 0)),
        out_shape=jax.ShapeDtypeStruct((t, D_MODEL), F32),
        scratch_shapes=[pltpu.VMEM((2 * TOP_K * tm * ROW_TILE, LANES), U32), pltpu.SemaphoreType.DMA((2,))],
        compiler_params=_params(1),
        name="combine",
    )(codes, codes, starts, gate, h, final_g, yb)


def _expert_changed(i, be_ref):
    return (i == 0) | (be_ref[i] != be_ref[jnp.maximum(i - 1, 0)])


def _gate_up_body(be_ref, nu_ref, run_ref, nxt_ref, last_ref, nr_ref, xs_ref, *rest):
    bias_refs = rest[:2 * STEP_BLOCKS]
    w_hbm, o_ref, wbuf_ref, wgb_ref, wlb_ref, sem = rest[2 * STEP_BLOCKS:]
    c = pl.program_id(0)

    def w_copy(e, chunk, part, slot):
        col = pl.multiple_of((part * N_FF_CHUNKS + chunk) * FF_CHUNK, FF_CHUNK)
        return pltpu.make_async_copy(w_hbm.at[e, :, pl.ds(col, FF_CHUNK)], wbuf_ref.at[slot, part], sem.at[slot, part])

    for b in range(STEP_BLOCKS):
        i = pl.program_id(1) * STEP_BLOCKS + b
        live = i < nu_ref[0]
        bg_ref, bl_ref = bias_refs[2 * b], bias_refs[2 * b + 1]
        rows = slice(b * EXPERT_ROWS, (b + 1) * EXPERT_ROWS)

        @pl.when(live & _expert_changed(i, be_ref))
        def _():
            slot = (c * nr_ref[0] + run_ref[i]) & 1

            @pl.when((c == 0) & (i == 0))
            def _():
                for part in range(2):
                    w_copy(be_ref[0], 0, part, 0).start()

            for part in range(2):
                w_copy(be_ref[i], c, part, slot).wait()

            @pl.when(jnp.logical_not((last_ref[i] == 1) & (c == N_FF_CHUNKS - 1)))
            def _():
                for part in range(2):
                    w_copy(nxt_ref[i], c + last_ref[i], part, 1 - slot).start()

            wgb_ref[...] = wbuf_ref[slot, 0].astype(BF16)
            wlb_ref[...] = wbuf_ref[slot, 1].astype(BF16)

        @pl.when(live)
        def _():
            lo, hi = _load_row_tiles(xs_ref, b * EXPERT_ROWS * ROW_TILE, EXPERT_ROWS)
            x = jnp.concatenate([lo, hi], axis=1).astype(BF16)
            glu = jnp.dot(x, wgb_ref[...], preferred_element_type=F32) + bg_ref[...]
            lin = jnp.dot(x, wlb_ref[...], preferred_element_type=F32) + bl_ref[...]
            glu = jnp.minimum(glu, SWIGLU_LIMIT)
            lin = jnp.clip(lin, -SWIGLU_LIMIT, SWIGLU_LIMIT)
            o_ref[rows, :] = (glu * jax.nn.sigmoid(SWIGLU_ALPHA * glu) * (lin + 1.0)).astype(BF16)

        @pl.when(jnp.logical_not(live))
        def _():
            o_ref[rows, :] = jnp.zeros((EXPERT_ROWS, FF_CHUNK), BF16)


def _run_tables(block_e, n_used):
    n_blocks = block_e.shape[0]
    idx = jnp.arange(n_blocks, dtype=I32)
    live = idx < n_used[0]
    first = live & ((idx == 0) | (block_e != jnp.roll(block_e, 1)))
    run = jnp.cumsum(first.astype(I32)) - 1
    first_pos = jnp.where(first, idx, n_blocks)
    later = jnp.concatenate([lax.cummin(first_pos, reverse=True)[1:], jnp.full((1,), n_blocks, I32)])
    is_last = later >= n_blocks
    nxt = jnp.where(is_last, block_e[0], block_e[jnp.minimum(later, n_blocks - 1)])
    return run.astype(I32), nxt.astype(I32), is_last.astype(I32), jnp.sum(first.astype(I32)).reshape(1)


def _live_block(i, nu):
    return jnp.minimum(i, nu[0] - 1)


def _live_step(s, nu):
    return jnp.minimum(s, (nu[0] - 1) // STEP_BLOCKS)


def _gate_up(tables, xs, w_gate_up, b_gate_up):
    n_blocks = xs.shape[0] // (EXPERT_ROWS * ROW_TILE)
    assert n_blocks % STEP_BLOCKS == 0

    def bias(b, part):
        return pl.BlockSpec((None, 1, FF_CHUNK),
                            lambda c, s, be, nu, *_: (be[_live_block(s * STEP_BLOCKS + b, nu)], 0,
                                                      part * N_FF_CHUNKS + c))

    grid_spec = pltpu.PrefetchScalarGridSpec(
        num_scalar_prefetch=len(tables),
        grid=(N_FF_CHUNKS, n_blocks // STEP_BLOCKS),
        in_specs=[
            pl.BlockSpec((STEP_BLOCKS * EXPERT_ROWS * ROW_TILE, LANES),
                         lambda c, s, be, nu, *_: (_live_step(s, nu), 0)),
            *[bias(b, part) for b in range(STEP_BLOCKS) for part in range(2)],
            pl.BlockSpec(memory_space=pl.ANY),
        ],
        out_specs=pl.BlockSpec((STEP_BLOCKS * EXPERT_ROWS, FF_CHUNK), lambda c, s, *_: (s, c)),
        scratch_shapes=[pltpu.VMEM((2, 2, D_MODEL, FF_CHUNK), F32),
                        pltpu.VMEM((D_MODEL, FF_CHUNK), BF16), pltpu.VMEM((D_MODEL, FF_CHUNK), BF16),
                        pltpu.SemaphoreType.DMA((2, 2))],
    )
    return pl.pallas_call(
        _gate_up_body,
        grid_spec=grid_spec,
        out_shape=jax.ShapeDtypeStruct((n_blocks * EXPERT_ROWS, D_FF), BF16),
        compiler_params=_params(2, VMEM_LIMIT),
        name="expert_gate_up",
    )(*tables, xs, *([b_gate_up] * (2 * STEP_BLOCKS)), w_gate_up)


def _down_body(be_ref, nu_ref, run_ref, nxt_ref, last_ref, nr_ref, a_ref, *rest):
    del nr_ref
    bias_refs = rest[:STEP_BLOCKS]
    w_hbm, o_ref, wbuf_ref, wb_ref, sem = rest[STEP_BLOCKS:]

    def w_copy(e, slot):
        return pltpu.make_async_copy(w_hbm.at[e], wbuf_ref.at[slot], sem.at[slot])

    for b in range(STEP_BLOCKS):
        i = pl.program_id(0) * STEP_BLOCKS + b
        live = i < nu_ref[0]
        b_ref = bias_refs[b]
        block_rows = EXPERT_ROWS * ROW_TILE

        @pl.when(live & _expert_changed(i, be_ref))
        def _():
            slot = run_ref[i] & 1

            @pl.when(i == 0)
            def _():
                w_copy(be_ref[0], 0).start()

            w_copy(be_ref[i], slot).wait()

            @pl.when(last_ref[i] == 0)
            def _():
                w_copy(nxt_ref[i], 1 - slot).start()

            wb_ref[...] = wbuf_ref[slot].astype(BF16)

        @pl.when(live)
        def _():
            a = a_ref[b * EXPERT_ROWS:(b + 1) * EXPERT_ROWS, :]
            y = jnp.dot(a, wb_ref[...], preferred_element_type=F32) + b_ref[...]
            _store_row_tiles(o_ref, b * block_rows, y[:, :PACKED], y[:, PACKED:])

        @pl.when(jnp.logical_not(live))
        def _():
            o_ref[b * block_rows:(b + 1) * block_rows, :] = jnp.zeros((block_rows, LANES), U32)


def _down(tables, act, w_down, b_down):
    n_blocks = act.shape[0] // EXPERT_ROWS
    assert n_blocks % STEP_BLOCKS == 0

    def bias(b):
        return pl.BlockSpec((None, 1, D_MODEL),
                            lambda s, be, nu, *_: (be[_live_block(s * STEP_BLOCKS + b, nu)], 0, 0))

    grid_spec = pltpu.PrefetchScalarGridSpec(
        num_scalar_prefetch=len(tables),
        grid=(n_blocks // STEP_BLOCKS,),
        in_specs=[
            pl.BlockSpec((STEP_BLOCKS * EXPERT_ROWS, D_FF), lambda s, be, nu, *_: (_live_step(s, nu), 0)),
            *[bias(b) for b in range(STEP_BLOCKS)],
            pl.BlockSpec(memory_space=pl.ANY),
        ],
        out_specs=pl.BlockSpec((STEP_BLOCKS * EXPERT_ROWS * ROW_TILE, LANES), lambda s, *_: (s, 0)),
        scratch_shapes=[pltpu.VMEM((2, D_FF, D_MODEL), F32), pltpu.VMEM((D_FF, D_MODEL), BF16),
                        pltpu.SemaphoreType.DMA((2,))],
    )
    return pl.pallas_call(
        _down_body,
        grid_spec=grid_spec,
        out_shape=jax.ShapeDtypeStruct((n_blocks * EXPERT_ROWS * ROW_TILE, LANES), U32),
        compiler_params=_params(1, VMEM_LIMIT),
        name="expert_down",
    )(*tables, act, *([b_down] * STEP_BLOCKS), w_down)


def kernel(x_prompt, x_sample, cache_k, cache_v, norm1_g, w_in, attn_sinks, rel_bias, gm_ln_g, gm_ln_b, gm_ws,
           gm_bs, w_branch_attn, w_branch_gm, w_out, norm2_g, router_w, router_b, w_gate_up, b_gate_up, w_down,
           b_down, final_g):
    depth = w_in.shape[0]
    assert depth == 1, "single-layer stack"
    batch, seq, _ = x_prompt.shape
    dec_batch, dec_seq, _ = x_sample.shape
    n_cache = cache_k.shape[2]
    assert n_cache == WINDOW and seq % WINDOW == 0 and CHUNK % dec_seq == 0
    t_p, t_s = batch * seq, dec_batch * dec_seq
    l = 0

    w_proj = _split_w_in(w_in[l])
    wba, wbg, wout = w_branch_attn[l].astype(BF16), w_branch_gm[l].astype(BF16), w_out[l].astype(BF16)
    rw = router_w[l].astype(BF16)
    g1, g2, fg = norm1_g[l][None], norm2_g[l][None], final_g[None]
    rb = router_b[l][None]
    ln_g, ln_b = gm_ln_g[l][None], gm_ln_b[l][None]
    reps = CHUNK // dec_seq
    ws_p, bs_p = gm_ws[l], gm_bs[l][:, :, None]
    ws_s = jnp.tile(gm_ws[l][:, :dec_seq, :dec_seq], (1, reps, reps))
    bs_s = jnp.tile(gm_bs[l][:, :dec_seq], (1, reps))[:, :, None]
    b_gu = b_gate_up[l][:, None, :]
    b_dn = b_down[l][:, None, :]
    sinks = attn_sinks[l]

    xp = x_prompt.reshape(t_p, D_MODEL)
    xs_ = x_sample.reshape(t_s, D_MODEL)

    raw_p, u_p, gates_p = _in_project(xp, g1, *w_proj)
    ao_p = _attention_prompt(raw_p, rel_bias, sinks, seq)
    (gm_p,) = _spatial_gating(u_p, raw_p, ln_g, ln_b, ws_p, bs_p, CHUNK, False)

    raw_s, u_s, gates_s = _in_project(xs_, g1, *w_proj)
    ao_s, k_s, v_s = _attention_sample(raw_s.reshape(dec_batch, dec_seq, RAW_WIDTH),
                                       cache_k[l].reshape(dec_batch, n_cache, KV_WIDTH),
                                       cache_v[l].reshape(dec_batch, n_cache, KV_WIDTH), rel_bias, sinks)
    gm_s, vn_s = _spatial_gating(u_s, raw_s, ln_g, ln_b, ws_s, bs_s, dec_seq, True)

    zero_counts = jnp.zeros((1, N_EXPERTS), I32)
    h_p, hnp_p, code_p, gate_p, counts_p = _block_tail(xp, ao_p, gm_p, gates_p, wba, wbg, wout, g2, rw, rb,
                                                       zero_counts)
    h_s, hnp_s, code_s, gate_s, counts = _block_tail(xs_, ao_s.reshape(t_s, ATTN_WIDTH), gm_s, gates_s, wba, wbg,
                                                     wout, g2, rw, rb, counts_p)
    code_p = code_p[:, :TOP_K].reshape(-1)
    code_s = code_s[:, :TOP_K].reshape(-1)

    counts = counts[0]
    padded = (counts + EXPERT_ROWS - 1) // EXPERT_ROWS * EXPERT_ROWS
    pad_end = jnp.cumsum(padded)
    pad_start = pad_end - padded
    n_blocks = (t_p + t_s) * TOP_K // EXPERT_ROWS + N_EXPERTS
    block_row = jnp.arange(n_blocks, dtype=I32) * EXPERT_ROWS
    block_e = jnp.minimum(jnp.sum(pad_end[None, :] <= block_row[:, None], axis=1), N_EXPERTS - 1).astype(I32)
    n_used = (pad_end[-1:] // EXPERT_ROWS).astype(I32)
    pad_start = pad_start.astype(I32)

    last_block = pad_end // EXPERT_ROWS - 1
    tail_block = n_used[0] + jnp.arange(N_EXPERTS, dtype=I32)
    cand = jnp.concatenate([last_block.astype(I32), tail_block])
    cand_ok = jnp.concatenate([padded > 0, tail_block < n_blocks])
    zero_blocks = cand[jnp.argsort(jnp.logical_not(cand_ok))]
    n_zero = jnp.sum(cand_ok.astype(I32)).reshape(1)
    tables = (block_e, n_used) + _run_tables(block_e, n_used)

    xs_sorted = _dispatch(jnp.concatenate([code_p, code_s]), pad_start, zero_blocks, n_zero, hnp_p, hnp_s,
                          n_blocks * EXPERT_ROWS)
    act = _gate_up(tables, xs_sorted, w_gate_up[l], b_gu)
    yb = _down(tables, act, w_down[l], b_dn)
    y_p = _combine(code_p, pad_start, gate_p, h_p, fg, yb)
    y_s = _combine(code_s, pad_start, gate_s, h_s, fg, yb)

    keep = min(WINDOW, seq)
    kv_last = raw_p.reshape(batch, seq, RAW_WIDTH)[:, -keep:, RAW_K_COL:]
    k_p = kv_last[:, :, :KV_WIDTH].reshape(batch, keep, N_KV_HEADS, HEAD_DIM)
    v_p = kv_last[:, :, KV_WIDTH:].reshape(batch, keep, N_KV_HEADS, HEAD_DIM)
    k_s = k_s.reshape(dec_batch, n_cache, N_KV_HEADS, HEAD_DIM)
    v_s = v_s.reshape(dec_batch, n_cache, N_KV_HEADS, HEAD_DIM)
    return (y_p.reshape(batch, seq, D_MODEL), y_s.reshape(dec_batch, dec_seq, D_MODEL),
            k_p[None], v_p[None], k_s[None], v_s[None], vn_s.reshape(1, dec_batch, dec_seq, GM_WIDTH))
```

```python
import functools
import math

import numpy as np
import jax
import jax.numpy as jnp
from jax import lax
from jax.experimental import pallas as pl
from jax.experimental.pallas import tpu as pltpu

F32 = jnp.float32
BF16 = jnp.bfloat16
U32 = jnp.uint32
I32 = jnp.int32

D_MODEL = 2048
HEAD_DIM = 64
N_HEADS = 16
N_KV_HEADS = 4
GQA_GROUP = N_HEADS // N_KV_HEADS
ATTN_WIDTH = N_HEADS * HEAD_DIM
KV_WIDTH = N_KV_HEADS * HEAD_DIM
WINDOW = 128
NUM_BUCKETS = 32
MAX_DISTANCE = 128
GM_WIDTH = D_MODEL // 2
GM_GROUPS = 8
CHUNK = 128
N_EXPERTS = 32
TOP_K = 4
D_FF = D_MODEL
SWIGLU_ALPHA = 1.702
SWIGLU_LIMIT = 7.0
NORM_EPS = 1e-5
NEG_INF = -1e30
IN_WIDTH = ATTN_WIDTH + 2 * KV_WIDTH + 2 * GM_WIDTH + 2 * D_MODEL

LANES = 128
IN_TM = 1024
IN_TN = 512
ACT_TN = 1024
GM_ROWS = 512
TAIL_TM = 256
EXPERT_ROWS = 256
FF_CHUNK = 1024
N_FF_CHUNKS = D_FF // FF_CHUNK
STEP_BLOCKS = 2
SAMPLE_SEQS = 8
SAMPLE_KEYS = 256
PACKED = D_MODEL // 2
RANK_BITS = 20
Q_BLOCKS = 2
ROW_TILE = PACKED // LANES
assert ROW_TILE == 8
VMEM_LIMIT = 56 * 1024 * 1024

RAW_WIDTH = ATTN_WIDTH + GM_WIDTH + 2 * KV_WIDTH
RAW_Q_BLOCK = 0
RAW_GV_BLOCK = ATTN_WIDTH // GM_WIDTH
RAW_KV_BLOCK = (ATTN_WIDTH + GM_WIDTH) // (2 * KV_WIDTH)
RAW_K_COL = ATTN_WIDTH + GM_WIDTH


def _params(n_axes, vmem=None):
    return pltpu.CompilerParams(dimension_semantics=("arbitrary",) * n_axes, vmem_limit_bytes=vmem)


def _pack_pair(lo, hi):
    lo_b = lax.bitcast_convert_type(lo.astype(BF16).astype(F32), U32)
    hi_b = lax.bitcast_convert_type(hi.astype(BF16).astype(F32), U32)
    return (lo_b >> 16) | (hi_b & jnp.uint32(0xFFFF0000))


def _unpack_pair(w):
    lo = lax.bitcast_convert_type(w << 16, F32)
    hi = lax.bitcast_convert_type(w & jnp.uint32(0xFFFF0000), F32)
    return lo, hi


def _store_row_tiles(ref, row0, lo, hi):
    n = lo.shape[0]
    words = _pack_pair(lo, hi)
    for s in range(ROW_TILE):
        ref[pl.ds(row0 + s, n, stride=ROW_TILE), :] = words[:, s * LANES:(s + 1) * LANES]


def _load_row_tiles(ref, row0, n):
    parts = [_unpack_pair(ref[pl.ds(row0 + s, n, stride=ROW_TILE), :]) for s in range(ROW_TILE)]
    return (jnp.concatenate([p[0] for p in parts], axis=1), jnp.concatenate([p[1] for p in parts], axis=1))


def _rms(x, g):
    return (x * lax.rsqrt(jnp.mean(x * x, axis=-1, keepdims=True) + NORM_EPS)) * g


def _proj_norm_body(x_ref, g_ref, w_ref, o_ref, xn_ref):
    @pl.when(pl.program_id(1) == 0)
    def _():
        xn_ref[...] = _rms(x_ref[...], g_ref[...]).astype(BF16)

    o_ref[...] = jnp.dot(xn_ref[...], w_ref[...], preferred_element_type=F32)


def _proj_body(xn_ref, w_ref, o_ref, *, epilogue):
    o_ref[...] = epilogue(jnp.dot(xn_ref[...], w_ref[...], preferred_element_type=F32)).astype(o_ref.dtype)


def _split_w_in(w_in):
    q_end = ATTN_WIDTH
    kv_end = q_end + 2 * KV_WIDTH
    gu_end = kv_end + GM_WIDTH
    gv_end = gu_end + GM_WIDTH
    w_raw = jnp.concatenate([w_in[:, :q_end], w_in[:, gu_end:gv_end], w_in[:, q_end:kv_end]], axis=1)
    return w_raw.astype(BF16), w_in[:, kv_end:gu_end].astype(BF16), w_in[:, gv_end:].astype(BF16)


def _in_project(x2d, g1, w_raw, w_gelu, w_sig):
    t = x2d.shape[0]

    raw, xn = pl.pallas_call(
        _proj_norm_body,
        grid=(t // IN_TM, RAW_WIDTH // IN_TN),
        in_specs=[pl.BlockSpec((IN_TM, D_MODEL), lambda i, j: (i, 0)),
                  pl.BlockSpec((1, D_MODEL), lambda i, j: (0, 0)),
                  pl.BlockSpec((D_MODEL, IN_TN), lambda i, j: (0, j))],
        out_specs=[pl.BlockSpec((IN_TM, IN_TN), lambda i, j: (i, j)),
                   pl.BlockSpec((IN_TM, D_MODEL), lambda i, j: (i, 0))],
        out_shape=[jax.ShapeDtypeStruct((t, RAW_WIDTH), F32), jax.ShapeDtypeStruct((t, D_MODEL), BF16)],
        compiler_params=_params(2, VMEM_LIMIT),
        name="in_project_raw",
    )(x2d, g1, w_raw)

    def act_call(w, epilogue, name):
        width = w.shape[1]
        return pl.pallas_call(
            functools.partial(_proj_body, epilogue=epilogue),
            grid=(t // IN_TM, width // ACT_TN),
            in_specs=[pl.BlockSpec((IN_TM, D_MODEL), lambda i, j: (i, 0)),
                      pl.BlockSpec((D_MODEL, ACT_TN), lambda i, j: (0, j))],
            out_specs=pl.BlockSpec((IN_TM, ACT_TN), lambda i, j: (i, j)),
            out_shape=jax.ShapeDtypeStruct((t, width), BF16),
            compiler_params=_params(2, VMEM_LIMIT),
            name=name,
        )(xn, w)

    return raw, act_call(w_gelu, jax.nn.gelu, "in_project_gelu"), act_call(w_sig, jax.nn.sigmoid, "in_project_sigmoid")


def _bucket_table(n_q, n_k, offset, n_valid_k):
    i = np.arange(n_q)[:, None]
    j = np.arange(n_k)[None, :]
    dist = offset + i - j
    n = np.maximum(dist, 0)
    max_exact = NUM_BUCKETS // 2
    nf = np.maximum(n, 1).astype(np.float64)
    large = max_exact + (np.log(nf / max_exact) / math.log(MAX_DISTANCE / max_exact)
                         * (NUM_BUCKETS - max_exact)).astype(np.int32)
    large = np.minimum(large, NUM_BUCKETS - 1)
    bucket = np.where(n < max_exact, n, large)
    valid = (dist >= 0) & (dist < WINDOW) & (j < n_valid_k)
    return jnp.asarray(np.where(valid, bucket, -1).astype(np.int32))


def _fill_bias(bkt_ref, rb_ref, bias_ref):
    bkt = bkt_ref[...]

    def per_head(h, carry):
        def per_bucket(b, acc):
            return jnp.where(bkt == b, rb_ref[b, h], acc)
        bias_ref[h] = lax.fori_loop(0, NUM_BUCKETS, per_bucket, jnp.full(bkt.shape, NEG_INF, F32))
        return carry

    lax.fori_loop(0, N_HEADS, per_head, 0)


def _sink_softmax(s, sink):
    m = jnp.maximum(jnp.max(s, axis=-1, keepdims=True), sink)
    p = jnp.exp(s - m)
    return p / (jnp.sum(p, axis=-1, keepdims=True) + jnp.exp(sink - m))


def _attn_prompt_body(q_ref, kvc_ref, kvp_ref, bkt_ref, rb_ref, sink_ref, o_ref, bias_ref, *, blocks_per_seq):
    s = pl.program_id(0)

    @pl.when(s == 0)
    def _():
        _fill_bias(bkt_ref, rb_ref, bias_ref)

    kv = jnp.concatenate([kvp_ref[...], kvc_ref[...]], axis=0)
    k3 = kv[:, :KV_WIDTH].astype(BF16)
    v3 = kv[:, KV_WIDTH:].astype(BF16)
    col = lax.broadcasted_iota(I32, (WINDOW, 2 * WINDOW), 1)
    for b in range(Q_BLOCKS):
        rows = slice(b * WINDOW, (b + 1) * WINDOW)
        first = ((s * Q_BLOCKS + b) % blocks_per_seq) == 0
        no_prev = (col < WINDOW) & first
        k2 = k3[b * WINDOW:(b + 2) * WINDOW]
        v2 = v3[b * WINDOW:(b + 2) * WINDOW]
        q = (q_ref[rows, :] * (HEAD_DIM ** -0.5)).astype(BF16)
        for kh in range(N_KV_HEADS):
            k_h = k2[:, kh * HEAD_DIM:(kh + 1) * HEAD_DIM]
            v_h = v2[:, kh * HEAD_DIM:(kh + 1) * HEAD_DIM]
            heads = [kh * GQA_GROUP + g for g in range(GQA_GROUP)]
            qg = jnp.concatenate([q[:, h * HEAD_DIM:(h + 1) * HEAD_DIM] for h in heads], axis=0)
            sc = lax.dot_general(qg, k_h, (((1,), (1,)), ((), ())), preferred_element_type=F32)
            probs = []
            for g, h in enumerate(heads):
                bias = jnp.where(no_prev, NEG_INF, bias_ref[h])
                probs.append(_sink_softmax(sc[g * WINDOW:(g + 1) * WINDOW] + bias, sink_ref[h]).astype(BF16))
            o = jnp.dot(jnp.concatenate(probs, axis=0), v_h, preferred_element_type=F32)
            for g, h in enumerate(heads):
                o_ref[rows, h * HEAD_DIM:(h + 1) * HEAD_DIM] = o[g * WINDOW:(g + 1) * WINDOW].astype(o_ref.dtype)


def _attention_prompt(raw, rel_bias, sinks, seq):
    t = raw.shape[0]
    step_rows = Q_BLOCKS * WINDOW
    assert t % step_rows == 0
    bkt = _bucket_table(WINDOW, 2 * WINDOW, WINDOW, 2 * WINDOW)
    smem = pl.BlockSpec(memory_space=pltpu.SMEM)
    return pl.pallas_call(
        functools.partial(_attn_prompt_body, blocks_per_seq=seq // WINDOW),
        grid=(t // step_rows,),
        in_specs=[
            pl.BlockSpec((step_rows, ATTN_WIDTH), lambda s: (s, RAW_Q_BLOCK)),
            pl.BlockSpec((step_rows, 2 * KV_WIDTH), lambda s: (s, RAW_KV_BLOCK)),
            pl.BlockSpec((WINDOW, 2 * KV_WIDTH), lambda s: (jnp.maximum(s * Q_BLOCKS - 1, 0), RAW_KV_BLOCK)),
            pl.BlockSpec((WINDOW, 2 * WINDOW), lambda s: (0, 0)),
            smem, smem,
        ],
        out_specs=pl.BlockSpec((step_rows, ATTN_WIDTH), lambda s: (s, 0)),
        out_shape=jax.ShapeDtypeStruct((t, ATTN_WIDTH), BF16),
        scratch_shapes=[pltpu.VMEM((N_HEADS, WINDOW, 2 * WINDOW), F32)],
        compiler_params=_params(1),
        name="attention_prompt",
    )(raw, raw, raw, bkt, rel_bias, sinks)


def _attn_sample_body(q_ref, kvn_ref, ck_ref, cv_ref, bkt_ref, rb_ref, sink_ref, o_ref, nk_ref, nv_ref, bias_ref,
                      *, n_q, n_cache):
    @pl.when(pl.program_id(0) == 0)
    def _():
        _fill_bias(bkt_ref, rb_ref, bias_ref)

    kvn = kvn_ref[...]
    zeros = jnp.zeros((SAMPLE_SEQS, SAMPLE_KEYS - n_cache - n_q, KV_WIDTH), F32)
    k_all = jnp.concatenate([ck_ref[...], kvn[:, :, :KV_WIDTH], zeros], axis=1)
    v_all = jnp.concatenate([cv_ref[...], kvn[:, :, KV_WIDTH:], zeros], axis=1)
    nk_ref[...] = k_all[:, n_q:n_q + n_cache]
    nv_ref[...] = v_all[:, n_q:n_q + n_cache]
    k = k_all.astype(BF16)
    v = v_all.astype(BF16)
    q = q_ref[...] * (HEAD_DIM ** -0.5)
    for kh in range(N_KV_HEADS):
        k_h = k[:, :, kh * HEAD_DIM:(kh + 1) * HEAD_DIM]
        v_h = v[:, :, kh * HEAD_DIM:(kh + 1) * HEAD_DIM]
        heads = [kh * GQA_GROUP + g for g in range(GQA_GROUP)]
        qg = jnp.concatenate([q[:, :, h * HEAD_DIM:(h + 1) * HEAD_DIM] for h in heads], axis=1).astype(BF16)
        sc = jnp.einsum('bqd,bsd->bqs', qg, k_h, preferred_element_type=F32)
        bias = jnp.concatenate([bias_ref[h] for h in heads], axis=0)
        sink = jnp.concatenate([jnp.full((n_q, 1), sink_ref[h], F32) for h in heads], axis=0)
        p = _sink_softmax(sc + bias[None], sink[None]).astype(BF16)
        o = jnp.einsum('bqs,bsd->bqd', p, v_h, preferred_element_type=F32)
        for g, h in enumerate(heads):
            o_ref[:, :, h * HEAD_DIM:(h + 1) * HEAD_DIM] = o[:, g * n_q:(g + 1) * n_q, :]


def _attention_sample(raw3, cache_k, cache_v, rel_bias, sinks):
    n_seq, n_q, _ = raw3.shape
    n_cache = cache_k.shape[1]
    bkt = _bucket_table(n_q, SAMPLE_KEYS, n_cache, n_cache + n_q)
    smem = pl.BlockSpec(memory_space=pltpu.SMEM)
    cache_spec = pl.BlockSpec((SAMPLE_SEQS, n_cache, KV_WIDTH), lambda s: (s, 0, 0))
    o_spec = pl.BlockSpec((SAMPLE_SEQS, n_q, ATTN_WIDTH), lambda s: (s, 0, 0))
    return pl.pallas_call(
        functools.partial(_attn_sample_body, n_q=n_q, n_cache=n_cache),
        grid=(n_seq // SAMPLE_SEQS,),
        in_specs=[pl.BlockSpec((SAMPLE_SEQS, n_q, ATTN_WIDTH), lambda s: (s, 0, RAW_Q_BLOCK)),
                  pl.BlockSpec((SAMPLE_SEQS, n_q, 2 * KV_WIDTH), lambda s: (s, 0, RAW_KV_BLOCK)),
                  cache_spec, cache_spec,
                  pl.BlockSpec((n_q, SAMPLE_KEYS), lambda s: (0, 0)), smem, smem],
        out_specs=[o_spec, cache_spec, cache_spec],
        out_shape=[jax.ShapeDtypeStruct((n_seq, n_q, ATTN_WIDTH), F32),
                   jax.ShapeDtypeStruct(cache_k.shape, F32), jax.ShapeDtypeStruct(cache_v.shape, F32)],
        scratch_shapes=[pltpu.VMEM((N_HEADS, n_q, SAMPLE_KEYS), F32)],
        compiler_params=_params(1),
        name="attention_sample",
    )(raw3, raw3, cache_k, cache_v, bkt, rel_bias, sinks)


def _gating_body(u_ref, gv_ref, lng_ref, lnb_ref, w_ref, bs_ref, *rest, seq_rows, emit_vn):
    if emit_vn:
        o_ref, vn_ref, wm_ref = rest
    else:
        o_ref, wm_ref = rest

    @pl.when(pl.program_id(0) == 0)
    def _():
        r = lax.broadcasted_iota(I32, (CHUNK, CHUNK), 0)
        c = lax.broadcasted_iota(I32, (CHUNK, CHUNK), 1)
        keep = c <= r
        if seq_rows < CHUNK:
            shift = seq_rows.bit_length() - 1
            assert seq_rows == 1 << shift
            keep = keep & ((r >> shift) == (c >> shift))
        for g in range(GM_GROUPS):
            wm_ref[g] = jnp.where(keep, w_ref[g], 0.0).astype(BF16)

    a = jax.nn.gelu(gv_ref[...])
    mu = jnp.mean(a, axis=-1, keepdims=True)
    var = jnp.mean(jnp.square(a - mu), axis=-1, keepdims=True)
    vn = ((a - mu) * lax.rsqrt(var + NORM_EPS)) * lng_ref[...] + lnb_ref[...]
    if emit_vn:
        vn_ref[...] = vn
    vnb = vn.astype(BF16)
    for ch in range(GM_ROWS // CHUNK):
        rows = slice(ch * CHUNK, (ch + 1) * CHUNK)
        for g in range(GM_GROUPS):
            cols = slice(g * CHUNK, (g + 1) * CHUNK)
            mixed = jnp.dot(wm_ref[g], vnb[rows, cols], preferred_element_type=F32) + bs_ref[g]
            o_ref[rows, cols] = (u_ref[rows, cols].astype(F32) * mixed).astype(BF16)


def _spatial_gating(u, raw, ln_g, ln_b, ws, bs_col, seq_rows, emit_vn):
    t = u.shape[0]
    row_spec = pl.BlockSpec((GM_ROWS, GM_WIDTH), lambda i: (i, 0))
    vec_spec = pl.BlockSpec((1, GM_WIDTH), lambda i: (0, 0))
    out_shape = [jax.ShapeDtypeStruct((t, GM_WIDTH), BF16)]
    out_specs = [row_spec]
    if emit_vn:
        out_shape.append(jax.ShapeDtypeStruct((t, GM_WIDTH), F32))
        out_specs.append(row_spec)
    return pl.pallas_call(
        functools.partial(_gating_body, seq_rows=seq_rows, emit_vn=emit_vn),
        grid=(t // GM_ROWS,),
        in_specs=[row_spec, pl.BlockSpec((GM_ROWS, GM_WIDTH), lambda i: (i, RAW_GV_BLOCK)), vec_spec, vec_spec,
                  pl.BlockSpec((GM_GROUPS, CHUNK, CHUNK), lambda i: (0, 0, 0)),
                  pl.BlockSpec((GM_GROUPS, CHUNK, 1), lambda i: (0, 0, 0))],
        out_specs=out_specs,
        out_shape=out_shape,
        scratch_shapes=[pltpu.VMEM((GM_GROUPS, CHUNK, CHUNK), BF16)],
        compiler_params=_params(1),
        name="spatial_gating",
    )(u, raw, ln_g, ln_b, ws, bs_col)


def _lanes4(vals, dtype):
    lane = lax.broadcasted_iota(I32, (vals[0].shape[0], LANES), 1)
    out = jnp.zeros((vals[0].shape[0], LANES), dtype)
    for k, v in enumerate(vals):
        out = jnp.where(lane == k, v.astype(dtype), out)
    return out


def _tail_body(x_ref, ao_ref, gm_ref, sga_ref, sgg_ref, wba_ref, wbg_ref, wout_ref, g2_ref, rw_ref, rb_ref,
               cin_ref, h_ref, hnp_ref, code_ref, gate_ref, cout_ref, carry_ref):
    @pl.when(pl.program_id(0) == 0)
    def _():
        carry_ref[...] = cin_ref[...].astype(F32)

    a = jnp.dot(ao_ref[...].astype(BF16), wba_ref[...], preferred_element_type=F32)
    m = jnp.dot(gm_ref[...], wbg_ref[...], preferred_element_type=F32)
    merged = sga_ref[...].astype(F32) * a + sgg_ref[...].astype(F32) * m
    h = x_ref[...] + jnp.dot(merged.astype(BF16), wout_ref[...], preferred_element_type=F32)
    h_ref[...] = h
    hn = _rms(h, g2_ref[...])
    _store_row_tiles(hnp_ref, 0, hn[:, :PACKED], hn[:, PACKED:])

    logits = jnp.dot(hn.astype(BF16), rw_ref[...], preferred_element_type=F32) + rb_ref[...]
    tm = logits.shape[0]
    lane = lax.broadcasted_iota(I32, (tm, N_EXPERTS), 1).astype(F32)
    work = logits
    tops, idxs, hots = [], [], []
    for _ in range(TOP_K):
        mx = jnp.max(work, axis=-1, keepdims=True)
        idx = jnp.min(jnp.where(work == mx, lane, float(N_EXPERTS)), axis=-1, keepdims=True)
        hot = lane == idx
        tops.append(mx)
        idxs.append(idx)
        hots.append(hot)
        work = jnp.where(hot, -jnp.inf, work)
    ex = [jnp.exp(v - tops[0]) for v in tops]
    den = ex[0] + ex[1] + ex[2] + ex[3]
    gates = [v / den for v in ex]

    onehot = sum(jnp.where(hot, 1.0, 0.0) for hot in hots)
    r = lax.broadcasted_iota(I32, (tm, tm), 0)
    c = lax.broadcasted_iota(I32, (tm, tm), 1)
    earlier = jnp.where(c < r, 1.0, 0.0).astype(BF16)
    prefix = jnp.dot(earlier, onehot.astype(BF16), preferred_element_type=F32) + carry_ref[...]
    ranks = [jnp.sum(jnp.where(hot, prefix, 0.0), axis=-1, keepdims=True) for hot in hots]
    carry_ref[...] = carry_ref[...] + jnp.sum(onehot, axis=0, keepdims=True)

    code_ref[...] = _lanes4([e.astype(I32) * (1 << RANK_BITS) + r.astype(I32) for e, r in zip(idxs, ranks)], I32)
    gate_ref[...] = _lanes4(gates, F32)
    cout_ref[...] = carry_ref[...].astype(I32)


def _block_tail(x2d, attn_o, gm_o, gates, wba, wbg, wout, g2, rw, rb, counts_in):
    t = x2d.shape[0]
    tm = TAIL_TM
    row = lambda w, col=0: pl.BlockSpec((tm, w), lambda i: (i, col))
    once = lambda shape: pl.BlockSpec(shape, lambda i: (0, 0), pipeline_mode=pl.Buffered(1))
    small = lambda shape: pl.BlockSpec(shape, lambda i: (0, 0))
    return pl.pallas_call(
        _tail_body,
        grid=(t // tm,),
        in_specs=[row(D_MODEL), row(ATTN_WIDTH), row(GM_WIDTH), row(D_MODEL, 0), row(D_MODEL, 1),
                  once((ATTN_WIDTH, D_MODEL)), once((GM_WIDTH, D_MODEL)), once((D_MODEL, D_MODEL)),
                  small((1, D_MODEL)), small((D_MODEL, N_EXPERTS)), small((1, N_EXPERTS)), small((1, N_EXPERTS))],
        out_specs=[row(D_MODEL), pl.BlockSpec((tm * ROW_TILE, LANES), lambda i: (i, 0)),
                   row(LANES), row(LANES), small((1, N_EXPERTS))],
        out_shape=[
            jax.ShapeDtypeStruct((t, D_MODEL), F32),
            jax.ShapeDtypeStruct((t * ROW_TILE, LANES), U32),
            jax.ShapeDtypeStruct((t, LANES), I32),
            jax.ShapeDtypeStruct((t, LANES), F32),
            jax.ShapeDtypeStruct((1, N_EXPERTS), I32),
        ],
        scratch_shapes=[pltpu.VMEM((1, N_EXPERTS), F32)],
        compiler_params=_params(1, VMEM_LIMIT),
        name="block_tail",
    )(x2d, attn_o, gm_o, gates, gates, wba, wbg, wout, g2, rw, rb, counts_in)


_WAIT_UNROLL = 16


def _wait_rows(make_copy, n):
    def body(_, carry):
        for _ in range(_WAIT_UNROLL):
            make_copy().wait()
        return carry
    lax.fori_loop(0, n // _WAIT_UNROLL, body, 0)


def _token_rows(ref, t):
    return ref.at[pl.ds(pl.multiple_of(t * ROW_TILE, ROW_TILE), ROW_TILE), :]


def _slot(code, start_ref):
    return start_ref[lax.shift_right_logical(code, RANK_BITS)] + (code & ((1 << RANK_BITS) - 1))


def _dispatch_body(code_ref, start_ref, zb_ref, nz_ref, hp_ref, hs_ref, xs_ref, zero_ref, sem, zsem, *, prompt_tiles):
    i = pl.program_id(0)
    tm = hp_ref.shape[0] // ROW_TILE
    block_rows = EXPERT_ROWS * ROW_TILE

    @pl.when(i == 0)
    def _():
        zero_ref[...] = jnp.zeros_like(zero_ref)

        def zero_copy(j):
            row = pl.multiple_of(zb_ref[j] * block_rows, block_rows)
            return pltpu.make_async_copy(zero_ref, xs_ref.at[pl.ds(row, block_rows), :], zsem)

        def start(j, carry):
            zero_copy(j).start()
            return carry

        def wait(j, carry):
            zero_copy(j).wait()
            return carry

        lax.fori_loop(0, nz_ref[0], start, 0)
        lax.fori_loop(0, nz_ref[0], wait, 0)

    def scatter(hn_ref):
        def issue(t, carry):
            for k in range(TOP_K):
                d = _slot(code_ref[t * TOP_K + k], start_ref)
                pltpu.make_async_copy(_token_rows(hn_ref, t), _token_rows(xs_ref, d), sem).start(priority=k % 2)
            return carry

        lax.fori_loop(0, tm, issue, 0, unroll=4)
        _wait_rows(lambda: pltpu.make_async_copy(_token_rows(hn_ref, 0), _token_rows(xs_ref, 0), sem), tm * TOP_K)

    @pl.when(i < prompt_tiles)
    def _():
        scatter(hp_ref)

    @pl.when(i >= prompt_tiles)
    def _():
        scatter(hs_ref)


def _dispatch(codes, starts, zero_blocks, n_zero, hnp_p, hnp_s, n_slots):
    tm = TAIL_TM
    tiles_p = hnp_p.shape[0] // (tm * ROW_TILE)
    tiles_s = hnp_s.shape[0] // (tm * ROW_TILE)
    smem = pl.BlockSpec(memory_space=pltpu.SMEM)
    return pl.pallas_call(
        functools.partial(_dispatch_body, prompt_tiles=tiles_p),
        grid=(tiles_p + tiles_s,),
        in_specs=[pl.BlockSpec((tm * TOP_K,), lambda i: (i,), memory_space=pltpu.SMEM), smem, smem, smem,
                  pl.BlockSpec((tm * ROW_TILE, LANES), lambda i: (jnp.minimum(i, tiles_p - 1), 0)),
                  pl.BlockSpec((tm * ROW_TILE, LANES), lambda i: (jnp.maximum(i - tiles_p, 0), 0))],
        out_specs=pl.BlockSpec(memory_space=pl.ANY),
        out_shape=jax.ShapeDtypeStruct((n_slots * ROW_TILE, LANES), U32),
        scratch_shapes=[pltpu.VMEM((EXPERT_ROWS * ROW_TILE, LANES), U32),
                        pltpu.SemaphoreType.DMA, pltpu.SemaphoreType.DMA],
        compiler_params=_params(1),
        name="dispatch",
    )(codes, starts, zero_blocks, n_zero, hnp_p, hnp_s)


def _combine_body(code_ref, next_code_ref, start_ref, gate_ref, h_ref, fg_ref, yb_ref, o_ref, buf_ref, sem):
    s = pl.program_id(0)
    tm = h_ref.shape[0]
    half = s & 1

    def gather(codes, into):
        base = into * (TOP_K * tm)

        def issue(t, carry):
            for k in range(TOP_K):
                d = _slot(codes[t * TOP_K + k], start_ref)
                pltpu.make_async_copy(_token_rows(yb_ref, d), _token_rows(buf_ref, base + k * tm + t),
                                      sem.at[into]).start(priority=k % 2)
            return carry

        lax.fori_loop(0, tm, issue, 0, unroll=4)

    @pl.when(s == 0)
    def _():
        gather(code_ref, 0)

    @pl.when(s + 1 < pl.num_programs(0))
    def _():
        gather(next_code_ref, 1 - half)

    _wait_rows(lambda: pltpu.make_async_copy(_token_rows(yb_ref, 0), _token_rows(buf_ref, 0), sem.at[half]),
               tm * TOP_K)

    gate = gate_ref[...]
    f_lo = jnp.zeros((tm, PACKED), F32)
    f_hi = jnp.zeros((tm, PACKED), F32)
    for k in range(TOP_K):
        lo, hi = _load_row_tiles(buf_ref, (half * TOP_K + k) * tm * ROW_TILE, tm)
        gk = gate[:, k:k + 1]
        f_lo = f_lo + gk * lo
        f_hi = f_hi + gk * hi
    y = h_ref[...] + jnp.concatenate([f_lo, f_hi], axis=1)
    o_ref[...] = _rms(y, fg_ref[...])


def _combine(codes, starts, gate, h, final_g, yb):
    t = h.shape[0]
    tm = TAIL_TM
    n_tiles = t // tm
    return pl.pallas_call(
        _combine_body,
        grid=(n_tiles,),
        in_specs=[pl.BlockSpec((tm * TOP_K,), lambda i: (i,), memory_space=pltpu.SMEM),
                  pl.BlockSpec((tm * TOP_K,), lambda i: (jnp.minimum(i + 1, n_tiles - 1),), memory_space=pltpu.SMEM),
                  pl.BlockSpec(memory_space=pltpu.SMEM),
                  pl.BlockSpec((tm, LANES), lambda i: (i, 0)),
                  pl.BlockSpec((tm, D_MODEL), lambda i: (i, 0)),
                  pl.BlockSpec((1, D_MODEL), lambda i: (0, 0)),
                  pl.BlockSpec(memory_space=pl.ANY)],
        out_specs=pl.BlockSpec((tm, D_MODEL), lambda i: (i, 0)),
        out_shape=jax.ShapeDtypeStruct((t, D_MODEL), F32),
        scratch_shapes=[pltpu.VMEM((2 * TOP_K * tm * ROW_TILE, LANES), U32), pltpu.SemaphoreType.DMA((2,))],
        compiler_params=_params(1),
        name="combine",
    )(codes, codes, starts, gate, h, final_g, yb)


def _expert_changed(i, be_ref):
    return (i == 0) | (be_ref[i] != be_ref[jnp.maximum(i - 1, 0)])


def _gate_up_body(be_ref, nu_ref, run_ref, nxt_ref, last_ref, nr_ref, xs_ref, *rest):
    bias_refs = rest[:2 * STEP_BLOCKS]
    w_hbm, o_ref, wbuf_ref, wgb_ref, wlb_ref, sem = rest[2 * STEP_BLOCKS:]
    c = pl.program_id(0)

    def w_copy(e, chunk, part, slot):
        col = pl.multiple_of((part * N_FF_CHUNKS + chunk) * FF_CHUNK, FF_CHUNK)
        return pltpu.make_async_copy(w_hbm.at[e, :, pl.ds(col, FF_CHUNK)], wbuf_ref.at[slot, part], sem.at[slot, part])

    for b in range(STEP_BLOCKS):
        i = pl.program_id(1) * STEP_BLOCKS + b
        live = i < nu_ref[0]
        bg_ref, bl_ref = bias_refs[2 * b], bias_refs[2 * b + 1]
        rows = slice(b * EXPERT_ROWS, (b + 1) * EXPERT_ROWS)

        @pl.when(live & _expert_changed(i, be_ref))
        def _():
            slot = (c * nr_ref[0] + run_ref[i]) & 1

            @pl.when((c == 0) & (i == 0))
            def _():
                for part in range(2):
                    w_copy(be_ref[0], 0, part, 0).start()

            for part in range(2):
                w_copy(be_ref[i], c, part, slot).wait()

            @pl.when(jnp.logical_not((last_ref[i] == 1) & (c == N_FF_CHUNKS - 1)))
            def _():
                for part in range(2):
                    w_copy(nxt_ref[i], c + last_ref[i], part, 1 - slot).start()

            wgb_ref[...] = wbuf_ref[slot, 0].astype(BF16)
            wlb_ref[...] = wbuf_ref[slot, 1].astype(BF16)

        @pl.when(live)
        def _():
            lo, hi = _load_row_tiles(xs_ref, b * EXPERT_ROWS * ROW_TILE, EXPERT_ROWS)
            x = jnp.concatenate([lo, hi], axis=1).astype(BF16)
            glu = jnp.dot(x, wgb_ref[...], preferred_element_type=F32) + bg_ref[...]
            lin = jnp.dot(x, wlb_ref[...], preferred_element_type=F32) + bl_ref[...]
            glu = jnp.minimum(glu, SWIGLU_LIMIT)
            lin = jnp.clip(lin, -SWIGLU_LIMIT, SWIGLU_LIMIT)
            o_ref[rows, :] = (glu * jax.nn.sigmoid(SWIGLU_ALPHA * glu) * (lin + 1.0)).astype(BF16)

        @pl.when(jnp.logical_not(live))
        def _():
            o_ref[rows, :] = jnp.zeros((EXPERT_ROWS, FF_CHUNK), BF16)


def _run_tables(block_e, n_used):
    n_blocks = block_e.shape[0]
    idx = jnp.arange(n_blocks, dtype=I32)
    live = idx < n_used[0]
    first = live & ((idx == 0) | (block_e != jnp.roll(block_e, 1)))
    run = jnp.cumsum(first.astype(I32)) - 1
    first_pos = jnp.where(first, idx, n_blocks)
    later = jnp.concatenate([lax.cummin(first_pos, reverse=True)[1:], jnp.full((1,), n_blocks, I32)])
    is_last = later >= n_blocks
    nxt = jnp.where(is_last, block_e[0], block_e[jnp.minimum(later, n_blocks - 1)])
    return run.astype(I32), nxt.astype(I32), is_last.astype(I32), jnp.sum(first.astype(I32)).reshape(1)


def _live_block(i, nu):
    return jnp.minimum(i, nu[0] - 1)


def _live_step(s, nu):
    return jnp.minimum(s, (nu[0] - 1) // STEP_BLOCKS)


def _gate_up(tables, xs, w_gate_up, b_gate_up):
    n_blocks = xs.shape[0] // (EXPERT_ROWS * ROW_TILE)
    assert n_blocks % STEP_BLOCKS == 0

    def bias(b, part):
        return pl.BlockSpec((None, 1, FF_CHUNK),
                            lambda c, s, be, nu, *_: (be[_live_block(s * STEP_BLOCKS + b, nu)], 0,
                                                      part * N_FF_CHUNKS + c))

    grid_spec = pltpu.PrefetchScalarGridSpec(
        num_scalar_prefetch=len(tables),
        grid=(N_FF_CHUNKS, n_blocks // STEP_BLOCKS),
        in_specs=[
            pl.BlockSpec((STEP_BLOCKS * EXPERT_ROWS * ROW_TILE, LANES),
                         lambda c, s, be, nu, *_: (_live_step(s, nu), 0)),
            *[bias(b, part) for b in range(STEP_BLOCKS) for part in range(2)],
            pl.BlockSpec(memory_space=pl.ANY),
        ],
        out_specs=pl.BlockSpec((STEP_BLOCKS * EXPERT_ROWS, FF_CHUNK), lambda c, s, *_: (s, c)),
        scratch_shapes=[pltpu.VMEM((2, 2, D_MODEL, FF_CHUNK), F32),
                        pltpu.VMEM((D_MODEL, FF_CHUNK), BF16), pltpu.VMEM((D_MODEL, FF_CHUNK), BF16),
                        pltpu.SemaphoreType.DMA((2, 2))],
    )
    return pl.pallas_call(
        _gate_up_body,
        grid_spec=grid_spec,
        out_shape=jax.ShapeDtypeStruct((n_blocks * EXPERT_ROWS, D_FF), BF16),
        compiler_params=_params(2, VMEM_LIMIT),
        name="expert_gate_up",
    )(*tables, xs, *([b_gate_up] * (2 * STEP_BLOCKS)), w_gate_up)


def _down_body(be_ref, nu_ref, run_ref, nxt_ref, last_ref, nr_ref, a_ref, *rest):
    del nr_ref
    bias_refs = rest[:STEP_BLOCKS]
    w_hbm, o_ref, wbuf_ref, wb_ref, sem = rest[STEP_BLOCKS:]

    def w_copy(e, slot):
        return pltpu.make_async_copy(w_hbm.at[e], wbuf_ref.at[slot], sem.at[slot])

    for b in range(STEP_BLOCKS):
        i = pl.program_id(0) * STEP_BLOCKS + b
        live = i < nu_ref[0]
        b_ref = bias_refs[b]
        block_rows = EXPERT_ROWS * ROW_TILE

        @pl.when(live & _expert_changed(i, be_ref))
        def _():
            slot = run_ref[i] & 1

            @pl.when(i == 0)
            def _():
                w_copy(be_ref[0], 0).start()

            w_copy(be_ref[i], slot).wait()

            @pl.when(last_ref[i] == 0)
            def _():
                w_copy(nxt_ref[i], 1 - slot).start()

            wb_ref[...] = wbuf_ref[slot].astype(BF16)

        @pl.when(live)
        def _():
            a = a_ref[b * EXPERT_ROWS:(b + 1) * EXPERT_ROWS, :]
            y = jnp.dot(a, wb_ref[...], preferred_element_type=F32) + b_ref[...]
            _store_row_tiles(o_ref, b * block_rows, y[:, :PACKED], y[:, PACKED:])

        @pl.when(jnp.logical_not(live))
        def _():
            o_ref[b * block_rows:(b + 1) * block_rows, :] = jnp.zeros((block_rows, LANES), U32)


def _down(tables, act, w_down, b_down):
    n_blocks = act.shape[0] // EXPERT_ROWS
    assert n_blocks % STEP_BLOCKS == 0

    def bias(b):
        return pl.BlockSpec((None, 1, D_MODEL),
                            lambda s, be, nu, *_: (be[_live_block(s * STEP_BLOCKS + b, nu)], 0, 0))

    grid_spec = pltpu.PrefetchScalarGridSpec(
        num_scalar_prefetch=len(tables),
        grid=(n_blocks // STEP_BLOCKS,),
        in_specs=[
            pl.BlockSpec((STEP_BLOCKS * EXPERT_ROWS, D_FF), lambda s, be, nu, *_: (_live_step(s, nu), 0)),
            *[bias(b) for b in range(STEP_BLOCKS)],
            pl.BlockSpec(memory_space=pl.ANY),
        ],
        out_specs=pl.BlockSpec((STEP_BLOCKS * EXPERT_ROWS * ROW_TILE, LANES), lambda s, *_: (s, 0)),
        scratch_shapes=[pltpu.VMEM((2, D_FF, D_MODEL), F32), pltpu.VMEM((D_FF, D_MODEL), BF16),
                        pltpu.SemaphoreType.DMA((2,))],
    )
    return pl.pallas_call(
        _down_body,
        grid_spec=grid_spec,
        out_shape=jax.ShapeDtypeStruct((n_blocks * EXPERT_ROWS * ROW_TILE, LANES), U32),
        compiler_params=_params(1, VMEM_LIMIT),
        name="expert_down",
    )(*tables, act, *([b_down] * STEP_BLOCKS), w_down)


def kernel(x_prompt, x_sample, cache_k, cache_v, norm1_g, w_in, attn_sinks, rel_bias, gm_ln_g, gm_ln_b, gm_ws,
           gm_bs, w_branch_attn, w_branch_gm, w_out, norm2_g, router_w, router_b, w_gate_up, b_gate_up, w_down,
           b_down, final_g):
    depth = w_in.shape[0]
    assert depth == 1, "single-layer stack"
    batch, seq, _ = x_prompt.shape
    dec_batch, dec_seq, _ = x_sample.shape
    n_cache = cache_k.shape[2]
    assert n_cache == WINDOW and seq % WINDOW == 0 and CHUNK % dec_seq == 0
    t_p, t_s = batch * seq, dec_batch * dec_seq
    l = 0

    w_proj = _split_w_in(w_in[l])
    wba, wbg, wout = w_branch_attn[l].astype(BF16), w_branch_gm[l].astype(BF16), w_out[l].astype(BF16)
    rw = router_w[l].astype(BF16)
    g1, g2, fg = norm1_g[l][None], norm2_g[l][None], final_g[None]
    rb = router_b[l][None]
    ln_g, ln_b = gm_ln_g[l][None], gm_ln_b[l][None]
    reps = CHUNK // dec_seq
    ws_p, bs_p = gm_ws[l], gm_bs[l][:, :, None]
    ws_s = jnp.tile(gm_ws[l][:, :dec_seq, :dec_seq], (1, reps, reps))
    bs_s = jnp.tile(gm_bs[l][:, :dec_seq], (1, reps))[:, :, None]
    b_gu = b_gate_up[l][:, None, :]
    b_dn = b_down[l][:, None, :]
    sinks = attn_sinks[l]

    xp = x_prompt.reshape(t_p, D_MODEL)
    xs_ = x_sample.reshape(t_s, D_MODEL)

    raw_p, u_p, gates_p = _in_project(xp, g1, *w_proj)
    ao_p = _attention_prompt(raw_p, rel_bias, sinks, seq)
    (gm_p,) = _spatial_gating(u_p, raw_p, ln_g, ln_b, ws_p, bs_p, CHUNK, False)

    raw_s, u_s, gates_s = _in_project(xs_, g1, *w_proj)
    ao_s, k_s, v_s = _attention_sample(raw_s.reshape(dec_batch, dec_seq, RAW_WIDTH),
                                       cache_k[l].reshape(dec_batch, n_cache, KV_WIDTH),
                                       cache_v[l].reshape(dec_batch, n_cache, KV_WIDTH), rel_bias, sinks)
    gm_s, vn_s = _spatial_gating(u_s, raw_s, ln_g, ln_b, ws_s, bs_s, dec_seq, True)

    zero_counts = jnp.zeros((1, N_EXPERTS), I32)
    h_p, hnp_p, code_p, gate_p, counts_p = _block_tail(xp, ao_p, gm_p, gates_p, wba, wbg, wout, g2, rw, rb,
                                                       zero_counts)
    h_s, hnp_s, code_s, gate_s, counts = _block_tail(xs_, ao_s.reshape(t_s, ATTN_WIDTH), gm_s, gates_s, wba, wbg,
                                                     wout, g2, rw, rb, counts_p)
    code_p = code_p[:, :TOP_K].reshape(-1)
    code_s = code_s[:, :TOP_K].reshape(-1)

    counts = counts[0]
    padded = (counts + EXPERT_ROWS - 1) // EXPERT_ROWS * EXPERT_ROWS
    pad_end = jnp.cumsum(padded)
    pad_start = pad_end - padded
    n_blocks = (t_p + t_s) * TOP_K // EXPERT_ROWS + N_EXPERTS
    block_row = jnp.arange(n_blocks, dtype=I32) * EXPERT_ROWS
    block_e = jnp.minimum(jnp.sum(pad_end[None, :] <= block_row[:, None], axis=1), N_EXPERTS - 1).astype(I32)
    n_used = (pad_end[-1:] // EXPERT_ROWS).astype(I32)
    pad_start = pad_start.astype(I32)

    last_block = pad_end // EXPERT_ROWS - 1
    tail_block = n_used[0] + jnp.arange(N_EXPERTS, dtype=I32)
    cand = jnp.concatenate([last_block.astype(I32), tail_block])
    cand_ok = jnp.concatenate([padded > 0, tail_block < n_blocks])
    zero_blocks = cand[jnp.argsort(jnp.logical_not(cand_ok))]
    n_zero = jnp.sum(cand_ok.astype(I32)).reshape(1)
    tables = (block_e, n_used) + _run_tables(block_e, n_used)

    xs_sorted = _dispatch(jnp.concatenate([code_p, code_s]), pad_start, zero_blocks, n_zero, hnp_p, hnp_s,
                          n_blocks * EXPERT_ROWS)
    act = _gate_up(tables, xs_sorted, w_gate_up[l], b_gu)
    yb = _down(tables, act, w_down[l], b_dn)
    y_p = _combine(code_p, pad_start, gate_p, h_p, fg, yb)
    y_s = _combine(code_s, pad_start, gate_s, h_s, fg, yb)

    keep = min(WINDOW, seq)
    kv_last = raw_p.reshape(batch, seq, RAW_WIDTH)[:, -keep:, RAW_K_COL:]
    k_p = kv_last[:, :, :KV_WIDTH].reshape(batch, keep, N_KV_HEADS, HEAD_DIM)
    v_p = kv_last[:, :, KV_WIDTH:].reshape(batch, keep, N_KV_HEADS, HEAD_DIM)
    k_s = k_s.reshape(dec_batch, n_cache, N_KV_HEADS, HEAD_DIM)
    v_s = v_s.reshape(dec_batch, n_cache, N_KV_HEADS, HEAD_DIM)
    return (y_p.reshape(batch, seq, D_MODEL), y_s.reshape(dec_batch, dec_seq, D_MODEL),
            k_p[None], v_p[None], k_s[None], v_s[None], vn_s.reshape(1, dec_batch, dec_seq, GM_WIDTH))
```
